```python
import jax, jax.numpy as jnp
from jax import lax
import numpy as np

D_MODEL = 1024
BATCH = 16
SEQ = 4096
DEPTH = 1

N_META = 16
EXPAND = 2
D_MIX = EXPAND * D_MODEL
POOL_WINDOWS = (2, 4, 8, 16)
N_POOL_GROUPS = len(POOL_WINDOWS)
D_POOL = D_MIX // 4
POOL_GROUP = D_POOL // N_POOL_GROUPS
MAX_WINDOW = max(POOL_WINDOWS)
D_SSM = D_MIX - D_POOL
SSM_HEAD_DIM = 64
N_SSM_HEADS = D_SSM // SSM_HEAD_DIM
N_SSM_GROUPS = 4
HEADS_PER_GROUP = N_SSM_HEADS // N_SSM_GROUPS
D_STATE = 128
CONV_WIDTH = 4
CHUNK = 128
D_XBC = D_SSM + 2 * N_SSM_GROUPS * D_STATE
D_IN_PROJ = D_POOL + D_SSM + D_XBC + N_SSM_HEADS
META_PAD = (-N_META) % CHUNK
D_FF = 4 * D_MODEL
EPS = 1e-5

kernel_name = "hymba_pool_ssd_hybrid"


def rms_norm(x, w):
    x32 = x.astype(jnp.float32)
    y = x32 * lax.rsqrt(jnp.mean(x32 * x32, axis=-1, keepdims=True) + EPS)
    return (y * w.astype(jnp.float32)).astype(x.dtype)


def pool_mixer(u, pool_w, pool_scale):
    bsz, L, _ = u.shape
    ug = u.astype(jnp.float32).reshape(bsz, L, N_POOL_GROUPS, POOL_GROUP)
    csum = jnp.cumsum(ug, axis=1)
    cp = jnp.pad(csum, ((0, 0), (MAX_WINDOW, 0), (0, 0), (0, 0)))
    pos = jnp.arange(L)
    pooled = []
    for gi, w in enumerate(POOL_WINDOWS):
        win_sum = cp[:, MAX_WINDOW:, gi] - cp[:, MAX_WINDOW - w:MAX_WINDOW - w + L, gi]
        count = jnp.minimum(pos + 1, w).astype(jnp.float32)[None, :, None]
        pooled.append(win_sum / count - ug[:, :, gi])
    pooled = jnp.stack(pooled, axis=2)
    mixed = jnp.einsum('blgc,gcd->blgd', pooled, pool_w.astype(jnp.float32))
    out = mixed.reshape(bsz, L, D_POOL) * pool_scale.astype(jnp.float32)
    return out.astype(u.dtype)


def causal_dwconv(x, w, b):
    y = lax.conv_general_dilated(
        x, w[:, None, :].astype(x.dtype), window_strides=(1,),
        padding=[(CONV_WIDTH - 1, 0)], dimension_numbers=('NWC', 'WIO', 'NWC'),
        feature_group_count=x.shape[-1])
    return y + b.astype(x.dtype)


def ssd_chunked(xs, dt, a, bs, cs):
    bsz, T = xs.shape[:2]
    nc = T // CHUNK
    G, R, P, N = N_SSM_GROUPS, HEADS_PER_GROUP, SSM_HEAD_DIM, D_STATE
    x = xs.reshape(bsz, nc, CHUNK, G, R, P)
    d = dt.reshape(bsz, nc, CHUNK, G, R)
    bc = bs.reshape(bsz, nc, CHUNK, G, N)
    cc = cs.reshape(bsz, nc, CHUNK, G, N)
    a_cs = jnp.cumsum(d * a, axis=2)
    xdt = x * d[..., None]
    causal = jnp.tril(jnp.ones((CHUNK, CHUNK), dtype=bool))
    seg = a_cs[:, :, :, None] - a_cs[:, :, None, :]
    decay = jnp.exp(jnp.where(causal[None, None, :, :, None, None], seg, -jnp.inf))
    cb = jnp.einsum('bclgn,bcsgn->bclsg', cc, bc)
    y_diag = jnp.einsum('bclsgr,bcsgrp->bclgrp', cb[..., None] * decay, xdt)
    decay_to_end = jnp.exp(a_cs[:, :, -1:] - a_cs)
    states = jnp.einsum('bclgn,bclgr,bclgrp->bcgrpn', bc, decay_to_end, xdt)
    chunk_decay = jnp.exp(a_cs[:, :, -1])

    def step(carry, inp):
        st, dec = inp
        return carry * dec[..., None, None] + st, carry

    init = jnp.zeros((bsz, G, R, P, N), dtype=xs.dtype)
    _, prev = lax.scan(step, init, (jnp.moveaxis(states, 1, 0), jnp.moveaxis(chunk_decay, 1, 0)))
    prev = jnp.moveaxis(prev, 0, 1)
    y_off = jnp.einsum('bclgn,bcgrpn,bclgr->bclgrp', cc, prev, jnp.exp(a_cs))
    return (y_diag + y_off).reshape(bsz, T, G, R, P)


def ssd_mixer(z, xbc, dt_raw, conv_w, conv_b, dt_bias, a_log, d_skip, ssm_norm_w):
    bsz, L, _ = z.shape
    G, R, P, N = N_SSM_GROUPS, HEADS_PER_GROUP, SSM_HEAD_DIM, D_STATE
    T = L + META_PAD
    xbc_p = jnp.pad(xbc, ((0, 0), (META_PAD, 0), (0, 0)))
    dt_p = jnp.pad(dt_raw, ((0, 0), (META_PAD, 0), (0, 0)))
    xbc_c = jax.nn.silu(causal_dwconv(xbc_p, conv_w, conv_b)).astype(jnp.float32)
    valid = (jnp.arange(T) >= META_PAD)[None, :, None]
    dt = jnp.where(valid, jax.nn.softplus(dt_p.astype(jnp.float32) + dt_bias.astype(jnp.float32)), 0.0)
    xs = xbc_c[..., :D_SSM].reshape(bsz, T, G, R, P)
    bs = xbc_c[..., D_SSM:D_SSM + G * N].reshape(bsz, T, G, N)
    cs = xbc_c[..., D_SSM + G * N:].reshape(bsz, T, G, N)
    a = -jnp.exp(a_log.astype(jnp.float32)).reshape(G, R)
    y = ssd_chunked(xs, dt.reshape(bsz, T, G, R), a, bs, cs)
    y = y + d_skip.astype(jnp.float32).reshape(G, R)[..., None] * xs
    y = y[:, META_PAD:].reshape(bsz, L, G, R * P)
    y = y * jax.nn.silu(z.astype(jnp.float32).reshape(bsz, L, G, R * P))
    y = y * lax.rsqrt(jnp.mean(y * y, axis=-1, keepdims=True) + EPS)
    y = y * ssm_norm_w.astype(jnp.float32).reshape(G, R * P)
    return y.reshape(bsz, L, D_SSM).astype(z.dtype)


def setup_inputs(seed: int = 0) -> dict:
    key = jax.random.key(seed)
    ks = jax.random.split(key, 20)
    f32 = jnp.float32
    nrm = lambda k, shape, s: jax.random.normal(k, shape, f32) * s
    dt0 = jnp.exp(jax.random.uniform(ks[8], (DEPTH, N_SSM_HEADS), f32, np.log(1e-3), np.log(1e-1)))
    return {
        "x": nrm(ks[0], (BATCH, SEQ, D_MODEL), 1.0),
        "meta": nrm(ks[1], (N_META, D_MODEL), 1.0),
        "norm_mix_w": 1.0 + nrm(ks[2], (DEPTH, D_MODEL), 0.05),
        "w_in": nrm(ks[3], (DEPTH, D_MODEL, D_IN_PROJ), D_MODEL ** -0.5),
        "pool_w": nrm(ks[4], (DEPTH, N_POOL_GROUPS, POOL_GROUP, POOL_GROUP), POOL_GROUP ** -0.5),
        "pool_scale": 1.0 + nrm(ks[5], (DEPTH, D_POOL), 0.1),
        "conv_w": nrm(ks[6], (DEPTH, CONV_WIDTH, D_XBC), CONV_WIDTH ** -0.5),
        "conv_b": nrm(ks[7], (DEPTH, D_XBC), 0.02),
        "dt_bias": dt0 + jnp.log(-jnp.expm1(-dt0)),
        "a_log": jnp.log(jax.random.uniform(ks[9], (DEPTH, N_SSM_HEADS), f32, 1.0, 16.0)),
        "d_skip": 1.0 + nrm(ks[10], (DEPTH, N_SSM_HEADS), 0.1),
        "ssm_norm_w": 1.0 + nrm(ks[11], (DEPTH, D_SSM), 0.05),
        "w_out": nrm(ks[12], (DEPTH, D_MIX, D_MODEL), D_MIX ** -0.5),
        "norm_ffn_w": 1.0 + nrm(ks[13], (DEPTH, D_MODEL), 0.05),
        "w_ff1": nrm(ks[14], (DEPTH, D_MODEL, D_FF), D_MODEL ** -0.5),
        "w_ff2": nrm(ks[15], (DEPTH, D_FF, D_MODEL), D_FF ** -0.5),
        "norm_f_w": 1.0 + nrm(ks[16], (D_MODEL,), 0.05),
    }


def reference(x, meta, norm_mix_w, w_in, pool_w, pool_scale, conv_w, conv_b, dt_bias, a_log,
              d_skip, ssm_norm_w, w_out, norm_ffn_w, w_ff1, w_ff2, norm_f_w):
    bsz = x.shape[0]
    h = jnp.concatenate(
        [jnp.broadcast_to(meta[None].astype(x.dtype), (bsz, N_META, D_MODEL)), x], axis=1)
    for i in range(DEPTH):
        hn = rms_norm(h, norm_mix_w[i])
        proj = jnp.einsum('bld,de->ble', hn, w_in[i])
        u = proj[..., :D_POOL]
        z = proj[..., D_POOL:D_POOL + D_SSM]
        xbc = proj[..., D_POOL + D_SSM:D_POOL + D_SSM + D_XBC]
        dt_raw = proj[..., D_POOL + D_SSM + D_XBC:]
        y_pool = pool_mixer(u, pool_w[i], pool_scale[i])
        y_ssm = ssd_mixer(z, xbc, dt_raw, conv_w[i], conv_b[i], dt_bias[i], a_log[i],
                          d_skip[i], ssm_norm_w[i])
        y = jnp.concatenate([y_pool, y_ssm], axis=-1)
        h = h + jnp.einsum('ble,ed->bld', y, w_out[i])
        hn = rms_norm(h, norm_ffn_w[i])
        ff = jnp.square(jax.nn.relu(jnp.einsum('bld,df->blf', hn, w_ff1[i])))
        h = h + jnp.einsum('blf,fd->bld', ff, w_ff2[i])
    out = rms_norm(h, norm_f_w)
    return out[:, N_META:]
```

```python
import functools

import jax
import jax.numpy as jnp
from jax import lax
from jax.experimental import pallas as pl
from jax.experimental.pallas import tpu as pltpu

D_MODEL = 1024
N_META = 16
POOL_WINDOWS = (2, 4, 8, 16)
D_POOL = 512
POOL_GROUP = 128
MAX_WINDOW = 16
D_SSM = 1536
HEAD_DIM = 64
N_HEADS = 24
N_GROUPS = 4
HEADS_PER_GROUP = 6
D_STATE = 128
CONV_WIDTH = 4
CHUNK = 128
D_XBC = 2560
D_MIX = 2048
D_FF = 4096
EPS = 1e-5

LANES = 128
SUBLANES = 8
DT_COLS = LANES
D_IN_PAD = D_POOL + D_SSM + D_XBC + DT_COLS
OFF_Z = D_POOL
OFF_XBC = D_POOL + D_SSM
OFF_DT = OFF_XBC + D_XBC
N_PAIRS = N_HEADS // 2
PAIRS_PER_GROUP = HEADS_PER_GROUP // 2
GROUP_WIDTH = HEADS_PER_GROUP * HEAD_DIM
CONV_TAIL = SUBLANES
CONV_COL_BLOCK = 512

TM_MIX = 256
TM_FFN = 256
VMEM_LIMIT = 56 * 1024 * 1024

F32 = jnp.float32
BF16 = jnp.bfloat16


def _dot(a, b):
    return jnp.dot(a, b, preferred_element_type=F32)


def _rms(x, w):
    return x * lax.rsqrt(jnp.mean(x * x, axis=-1, keepdims=True) + EPS) * w


def _silu(x):
    return x * (1.0 / (1.0 + jnp.exp(-x)))


def _softplus(x):
    return jnp.maximum(x, 0.0) + jnp.log(1.0 + jnp.exp(-jnp.abs(x)))


def _split3(x):
    hi = x.astype(BF16)
    r1 = x - hi.astype(F32)
    mid = r1.astype(BF16)
    lo = (r1 - mid.astype(F32)).astype(BF16)
    return hi, mid, lo


def _bcast_rows(row):
    return jnp.broadcast_to(row, (CHUNK, LANES))


def _ssd_chunk(dt_c, a_row, xs_f32, xs_bf, b_c, c_c, gate_c, dskip_ref, ssmw_ref, s_ref, y_out):
    row_id = lax.broadcasted_iota(jnp.int32, (CHUNK, CHUNK), 0)
    col_id = lax.broadcasted_iota(jnp.int32, (CHUNK, CHUNK), 1)
    causal = row_id >= col_id
    lower_half = col_id < HEAD_DIM
    ltri = jnp.where(causal, 1.0, 0.0).astype(BF16)

    hi, mid, lo = _split3(dt_c * a_row)
    a_cs = _dot(ltri, hi) + _dot(ltri, mid) + _dot(ltri, lo)
    a_cs_t = a_cs.T
    dt_t = dt_c.T
    to_end_t = jnp.exp(a_cs_t[:, CHUNK - 1:CHUNK] - a_cs_t) * dt_t

    for g in range(N_GROUPS):
        gs = slice(g * D_STATE, (g + 1) * D_STATE)
        b_g = b_c[:, gs]
        c_g = c_c[:, gs]
        b_t = b_g.T
        if y_out is not None:
            cb = lax.dot_general(c_g.astype(BF16), b_g.astype(BF16),
                                 (((1,), (1,)), ((), ())), preferred_element_type=F32)
        y_pairs = []
        for q in range(PAIRS_PER_GROUP):
            pair = g * PAIRS_PER_GROUP + q
            ps = slice(pair * LANES, (pair + 1) * LANES)
            x_pair = xs_bf[:, ps]
            s_old = s_ref[pair]
            rhs = jnp.concatenate([x_pair, s_old.astype(BF16)], axis=0)
            b_scaled, y_full, end_decay = [], [], []
            for half in range(2):
                h = 2 * pair + half
                col_b = jnp.broadcast_to(a_cs[:, h:h + 1], (CHUNK, LANES))
                end_decay.append(jnp.exp(col_b[CHUNK - 1:CHUNK, :]))
                b_scaled.append((b_t * _bcast_rows(to_end_t[h:h + 1, :])).astype(BF16))
                if y_out is not None:
                    seg = jnp.where(causal, col_b - _bcast_rows(a_cs_t[h:h + 1, :]), -jnp.inf)
                    m_h = cb * jnp.exp(seg) * _bcast_rows(dt_t[h:h + 1, :])
                    c_h = c_g * jnp.exp(col_b)
                    lhs = jnp.concatenate([m_h.astype(BF16), c_h.astype(BF16)], axis=1)
                    y_full.append(_dot(lhs, rhs))
            zeros = jnp.zeros_like(x_pair)
            x_split = jnp.concatenate([jnp.where(lower_half, x_pair, zeros),
                                       jnp.where(lower_half, zeros, x_pair)], axis=0)
            st = _dot(jnp.concatenate(b_scaled, axis=1), x_split)
            s_ref[pair] = s_old * jnp.where(lower_half[0:1, :], end_decay[0], end_decay[1]) + st
            if y_out is not None:
                y_pairs.append(jnp.where(lower_half, y_full[0], y_full[1])
                               + dskip_ref[:, ps] * xs_f32[:, ps])
        if y_out is not None:
            ws = slice(g * GROUP_WIDTH, (g + 1) * GROUP_WIDTH)
            y_g = jnp.concatenate(y_pairs, axis=1) * gate_c[:, ws]
            y_g = y_g * lax.rsqrt(jnp.mean(y_g * y_g, axis=-1, keepdims=True) + EPS) * ssmw_ref[:, ws]
            y_out(g, y_g.astype(BF16))


def _project_and_conv(hn, tm, win_ref, convw_ref, convb_ref, dtb_ref, alog_ref,
                      xbuf, xs_f32, xs_bf, b_buf, c_buf):
    xbuf[CONV_TAIL:CONV_TAIL + tm, :] = _dot(hn, win_ref[:, OFF_XBC:OFF_DT])
    for cb in range(0, D_XBC, CONV_COL_BLOCK):
        cols = slice(cb, cb + CONV_COL_BLOCK)
        acc = convb_ref[:, cols] + convw_ref[CONV_WIDTH - 1:CONV_WIDTH, cols] * xbuf[CONV_TAIL:CONV_TAIL + tm, cols]
        for k in range(CONV_WIDTH - 1):
            start = CONV_TAIL - (CONV_WIDTH - 1) + k
            acc = acc + convw_ref[k:k + 1, cols] * xbuf[start:start + tm, cols]
        v = _silu(acc)
        if cb < D_SSM:
            xs_f32[:, cols] = v
            xs_bf[:, cols] = v.astype(BF16)
        elif cb < D_SSM + N_GROUPS * D_STATE:
            b_buf[:, cb - D_SSM:cb - D_SSM + CONV_COL_BLOCK] = v
        else:
            off = cb - D_SSM - N_GROUPS * D_STATE
            c_buf[:, off:off + CONV_COL_BLOCK] = v
    tail = xbuf[tm:tm + CONV_TAIL, :]
    xbuf[0:CONV_TAIL, :] = tail
    dt = _softplus(_dot(hn, win_ref[:, OFF_DT:D_IN_PAD]) + dtb_ref[...])
    a_row = -jnp.exp(alog_ref[...])
    return dt, a_row


def _meta_kernel(x_ref, nw_ref, win_ref, convw_ref, convb_ref, dtb_ref, alog_ref,
                 s_out, xtail_out, utail_out,
                 xbuf, xs_f32, xs_bf, b_buf, c_buf):
    s_out[...] = jnp.zeros_like(s_out)
    xbuf[0:CONV_TAIL, :] = jnp.zeros((CONV_TAIL, D_XBC), F32)
    hn = _rms(x_ref[...], nw_ref[...]).astype(BF16)
    utail_out[...] = _dot(hn, win_ref[:, 0:D_POOL])[CHUNK - MAX_WINDOW:CHUNK, :]
    dt, a_row = _project_and_conv(hn, CHUNK, win_ref, convw_ref, convb_ref, dtb_ref, alog_ref,
                                  xbuf, xs_f32, xs_bf, b_buf, c_buf)
    xtail_out[...] = xbuf[0:CONV_TAIL, :]
    valid = lax.broadcasted_iota(jnp.int32, (CHUNK, DT_COLS), 0) >= CHUNK - N_META
    dt = jnp.where(valid, dt, 0.0)
    _ssd_chunk(dt, a_row, None, xs_bf[...], b_buf[...], c_buf[...], None, None, None, s_out, None)


def _mixer_kernel(x_ref, nw_ref, win_ref, poolw_ref, pscale_ref, convw_ref, convb_ref, dtb_ref, alog_ref,
                  dskip_ref, ssmw_ref, wout_ref, s0_ref, xtail0_ref, utail0_ref,
                  h1_ref,
                  s_ref, ubuf, xbuf, xs_f32, xs_bf, b_buf, c_buf, gate_buf, y_buf):
    tm = TM_MIX

    @pl.when(pl.program_id(1) == 0)
    def _():
        s_ref[...] = s0_ref[...]
        ubuf[0:MAX_WINDOW, :] = utail0_ref[...]
        xbuf[0:CONV_TAIL, :] = xtail0_ref[...]

    x = x_ref[0]
    hn = _rms(x, nw_ref[...]).astype(BF16)

    ubuf[MAX_WINDOW:MAX_WINDOW + tm, :] = _dot(hn, win_ref[:, 0:D_POOL])
    for gi, w in enumerate(POOL_WINDOWS):
        cols = slice(gi * POOL_GROUP, (gi + 1) * POOL_GROUP)
        cur = ubuf[MAX_WINDOW:MAX_WINDOW + tm, cols]
        acc = cur
        for j in range(1, w):
            acc = acc + ubuf[MAX_WINDOW - j:MAX_WINDOW - j + tm, cols]
        pooled = acc * (1.0 / w) - cur
        mixed = _dot(pooled.astype(BF16), poolw_ref[gi])
        y_buf[:, cols] = (mixed * pscale_ref[:, cols]).astype(BF16)
    utail = ubuf[tm:tm + MAX_WINDOW, :]
    ubuf[0:MAX_WINDOW, :] = utail

    gate_buf[...] = _silu(_dot(hn, win_ref[:, OFF_Z:OFF_XBC]))
    dt, a_row = _project_and_conv(hn, tm, win_ref, convw_ref, convb_ref, dtb_ref, alog_ref,
                                  xbuf, xs_f32, xs_bf, b_buf, c_buf)

    for c in range(tm // CHUNK):
        rows = slice(c * CHUNK, (c + 1) * CHUNK)

        def store_y(g, val, rows=rows):
            y_buf[rows, D_POOL + g * GROUP_WIDTH:D_POOL + (g + 1) * GROUP_WIDTH] = val

        _ssd_chunk(dt[rows, :], a_row, xs_f32[rows, :], xs_bf[rows, :], b_buf[rows, :], c_buf[rows, :],
                   gate_buf[rows, :], dskip_ref, ssmw_ref, s_ref, store_y)

    h1_ref[0] = x + _dot(y_buf[...], wout_ref[...])


def _ffn_kernel(h_ref, nw_ref, w1_ref, w2_ref, nfw_ref, o_ref):
    h = h_ref[...]
    hn = _rms(h, nw_ref[...]).astype(BF16)
    act = jnp.square(jnp.maximum(_dot(hn, w1_ref[...]), 0.0)).astype(BF16)
    o_ref[...] = _rms(h + _dot(act, w2_ref[...]), nfw_ref[...])


def _const_spec(shape):
    nd = len(shape)
    return pl.BlockSpec(shape, lambda *_: (0,) * nd, pipeline_mode=pl.Buffered(1))


def kernel(x, meta, norm_mix_w, w_in, pool_w, pool_scale, conv_w, conv_b, dt_bias, a_log, d_skip,
           ssm_norm_w, w_out, norm_ffn_w, w_ff1, w_ff2, norm_f_w):
    bsz, seq, d_model = x.shape
    assert d_model == D_MODEL and seq % TM_MIX == 0 and (bsz * seq) % TM_FFN == 0
    assert w_in.shape == (1, D_MODEL, OFF_DT + N_HEADS) and meta.shape == (N_META, D_MODEL)

    win = jnp.concatenate([w_in[0], jnp.zeros((D_MODEL, DT_COLS - N_HEADS), F32)], axis=1).astype(BF16)
    pad_heads = lambda v: jnp.concatenate([v, jnp.zeros((DT_COLS - N_HEADS,), F32)])[None, :]
    dtb = pad_heads(dt_bias[0])
    alog = pad_heads(a_log[0])
    dskip = jnp.repeat(d_skip[0], HEAD_DIM)[None, :]
    row = lambda v: v[None, :]
    nw = row(norm_mix_w[0])
    convw = conv_w[0]
    convb = row(conv_b[0])
    meta_pad = jnp.concatenate([jnp.zeros((CHUNK - N_META, D_MODEL), F32), meta.astype(F32)], axis=0)

    state_shape = (N_PAIRS, D_STATE, LANES)
    s0, xtail0, utail0 = pl.pallas_call(
        _meta_kernel,
        out_shape=(jax.ShapeDtypeStruct(state_shape, F32),
                   jax.ShapeDtypeStruct((CONV_TAIL, D_XBC), F32),
                   jax.ShapeDtypeStruct((MAX_WINDOW, D_POOL), F32)),
        scratch_shapes=[
            pltpu.VMEM((CHUNK + CONV_TAIL, D_XBC), F32),
            pltpu.VMEM((CHUNK, D_SSM), F32),
            pltpu.VMEM((CHUNK, D_SSM), BF16),
            pltpu.VMEM((CHUNK, N_GROUPS * D_STATE), F32),
            pltpu.VMEM((CHUNK, N_GROUPS * D_STATE), F32),
        ],
        compiler_params=pltpu.CompilerParams(vmem_limit_bytes=VMEM_LIMIT),
        name="meta_prologue",
    )(meta_pad, nw, win, convw, convb, dtb, alog)

    tm = TM_MIX
    h1 = pl.pallas_call(
        _mixer_kernel,
        out_shape=jax.ShapeDtypeStruct((bsz, seq, D_MODEL), F32),
        grid=(bsz, seq // tm),
        in_specs=[
            pl.BlockSpec((1, tm, D_MODEL), lambda b, t: (b, t, 0)),
            _const_spec((1, D_MODEL)),
            _const_spec((D_MODEL, D_IN_PAD)),
            _const_spec((N_GROUPS, POOL_GROUP, POOL_GROUP)),
            _const_spec((1, D_POOL)),
            _const_spec((CONV_WIDTH, D_XBC)),
            _const_spec((1, D_XBC)),
            _const_spec((1, DT_COLS)),
            _const_spec((1, DT_COLS)),
            _const_spec((1, D_SSM)),
            _const_spec((1, D_SSM)),
            _const_spec((D_MIX, D_MODEL)),
            _const_spec(state_shape),
            _const_spec((CONV_TAIL, D_XBC)),
            _const_spec((MAX_WINDOW, D_POOL)),
        ],
        out_specs=pl.BlockSpec((1, tm, D_MODEL), lambda b, t: (b, t, 0)),
        scratch_shapes=[
            pltpu.VMEM(state_shape, F32),
            pltpu.VMEM((tm + MAX_WINDOW, D_POOL), F32),
            pltpu.VMEM((tm + CONV_TAIL, D_XBC), F32),
            pltpu.VMEM((tm, D_SSM), F32),
            pltpu.VMEM((tm, D_SSM), BF16),
            pltpu.VMEM((tm, N_GROUPS * D_STATE), F32),
            pltpu.VMEM((tm, N_GROUPS * D_STATE), F32),
            pltpu.VMEM((tm, D_SSM), F32),
            pltpu.VMEM((tm, D_MIX), BF16),
        ],
        compiler_params=pltpu.CompilerParams(
            dimension_semantics=("arbitrary", "arbitrary"), vmem_limit_bytes=VMEM_LIMIT),
        name="mixer",
    )(x, nw, win, pool_w[0].astype(BF16), row(pool_scale[0]), convw, convb, dtb, alog, dskip,
      row(ssm_norm_w[0]), w_out[0].astype(BF16), s0, xtail0, utail0)

    n_tok = bsz * seq
    out = pl.pallas_call(
        _ffn_kernel,
        out_shape=jax.ShapeDtypeStruct((n_tok, D_MODEL), F32),
        grid=(n_tok // TM_FFN,),
        in_specs=[
            pl.BlockSpec((TM_FFN, D_MODEL), lambda i: (i, 0)),
            _const_spec((1, D_MODEL)),
            _const_spec((D_MODEL, D_FF)),
            _const_spec((D_FF, D_MODEL)),
            _const_spec((1, D_MODEL)),
        ],
        out_specs=pl.BlockSpec((TM_FFN, D_MODEL), lambda i: (i, 0)),
        compiler_params=pltpu.CompilerParams(
            dimension_semantics=("arbitrary",), vmem_limit_bytes=VMEM_LIMIT),
        name="ffn",
    )(h1.reshape(n_tok, D_MODEL), row(norm_ffn_w[0]), w_ff1[0].astype(BF16), w_ff2[0].astype(BF16),
      row(norm_f_w))
    return out.reshape(bsz, seq, D_MODEL)
```

```python
import functools

import jax
import jax.numpy as jnp
from jax import lax
from jax.experimental import pallas as pl
from jax.experimental.pallas import tpu as pltpu

D_MODEL = 1024
N_META = 16
POOL_WINDOWS = (2, 4, 8, 16)
D_POOL = 512
POOL_GROUP = 128
MAX_WINDOW = 16
D_SSM = 1536
HEAD_DIM = 64
N_HEADS = 24
N_GROUPS = 4
HEADS_PER_GROUP = 6
D_STATE = 128
CONV_WIDTH = 4
CHUNK = 128
D_XBC = 2560
D_MIX = 2048
D_FF = 4096
EPS = 1e-5

LANES = 128
SUBLANES = 8
DT_COLS = LANES
D_IN_PAD = D_POOL + D_SSM + D_XBC + DT_COLS
OFF_Z = D_POOL
OFF_XBC = D_POOL + D_SSM
OFF_DT = OFF_XBC + D_XBC
N_PAIRS = N_HEADS // 2
PAIRS_PER_GROUP = HEADS_PER_GROUP // 2
GROUP_WIDTH = HEADS_PER_GROUP * HEAD_DIM
CONV_TAIL = SUBLANES
CONV_COL_BLOCK = 512
GATE_COL_BLOCK = 512

TM_MIX = 256
W_COL_BLOCK = 512
FF_BLOCK = W_COL_BLOCK
MLP_POINTS = (1,) + (0, 0, 0, 0, 0) + (1, 1, 1, 1) + (0, 1, 1, 1)
assert sum(MLP_POINTS) == D_FF // FF_BLOCK
VMEM_LIMIT = 60 * 1024 * 1024

F32 = jnp.float32
BF16 = jnp.bfloat16
LOG2E = 1.4426950408889634


def _dot(a, b):
    return jnp.dot(a, b, preferred_element_type=F32)


def _dot_split_n(a, w_ref, rows):
    return jnp.concatenate([_dot(a, w_ref[i, rows, :]) for i in range(w_ref.shape[0])], axis=1)


def _rms(x, w):
    return x * lax.rsqrt(jnp.mean(x * x, axis=-1, keepdims=True) + EPS) * w


def _silu(x):
    return x * (1.0 / (1.0 + jnp.exp(-x)))


def _softplus(x):
    return jnp.maximum(x, 0.0) + jnp.log(1.0 + jnp.exp(-jnp.abs(x)))


def _split3(x):
    hi = x.astype(BF16)
    r1 = x - hi.astype(F32)
    mid = r1.astype(BF16)
    lo = (r1 - mid.astype(F32)).astype(BF16)
    return hi, mid, lo


def _bcast_rows(row):
    return jnp.broadcast_to(row, (CHUNK, LANES))


def _causal_mask():
    row_id = lax.broadcasted_iota(jnp.int32, (CHUNK, CHUNK), 0)
    col_id = lax.broadcasted_iota(jnp.int32, (CHUNK, CHUNK), 1)
    return row_id >= col_id


def _ssd_decays(dt_c, a_row):
    ltri = jnp.where(_causal_mask(), 1.0, 0.0).astype(BF16)
    hi, mid, lo = _split3(dt_c * (a_row * LOG2E))
    a_cs = _dot(ltri, hi) + _dot(ltri, mid) + _dot(ltri, lo)
    a_cs_t = a_cs.T
    src_t = a_cs_t - jnp.log2(dt_c.T)
    to_end_t = jnp.exp2(a_cs_t[:, CHUNK - 1:CHUNK] - src_t)
    return a_cs, src_t, to_end_t


def _ssd_chunk(decays, xs_f32, xs_bf, b_c, c_c, gate_c, dskip_ref, ssmw_ref, s_ref, y_out,
               between=lambda: None):
    a_cs, src_t, to_end_t = decays
    causal = _causal_mask()
    lower_half = lax.broadcasted_iota(jnp.int32, (CHUNK, LANES), 1) < HEAD_DIM

    for g in range(N_GROUPS):
        between()
        gs = slice(g * D_STATE, (g + 1) * D_STATE)
        b_g = b_c[:, gs]
        c_g = c_c[:, gs]
        b_t = b_g.T
        if y_out is not None:
            cb = lax.dot_general(c_g.astype(BF16), b_g.astype(BF16),
                                 (((1,), (1,)), ((), ())), preferred_element_type=F32)
        y_pairs = []
        for q in range(PAIRS_PER_GROUP):
            pair = g * PAIRS_PER_GROUP + q
            ps = slice(pair * LANES, (pair + 1) * LANES)
            x_pair = xs_bf[:, ps]
            s_old = s_ref[pair]
            rhs = jnp.concatenate([x_pair, s_old.astype(BF16)], axis=0)
            b_scaled, lhs, end_decay = [], [], []
            for half in range(2):
                h = 2 * pair + half
                col_b = jnp.broadcast_to(a_cs[:, h:h + 1], (CHUNK, LANES))
                end_decay.append(jnp.exp2(col_b[CHUNK - 1:CHUNK, :]))
                b_scaled.append((b_t * _bcast_rows(to_end_t[h:h + 1, :])).astype(BF16))
                if y_out is not None:
                    seg = jnp.where(causal, col_b - _bcast_rows(src_t[h:h + 1, :]), -jnp.inf)
                    m_h = cb * jnp.exp2(seg)
                    c_h = c_g * jnp.exp2(col_b)
                    lhs.append(jnp.concatenate([m_h.astype(BF16), c_h.astype(BF16)], axis=1))
            if y_out is not None:
                y_both = _dot(jnp.concatenate(lhs, axis=0), rhs)
                y_full = (y_both[0:CHUNK, :], y_both[CHUNK:2 * CHUNK, :])
            zeros = jnp.zeros_like(x_pair)
            x_split = jnp.concatenate([jnp.where(lower_half, x_pair, zeros),
                                       jnp.where(lower_half, zeros, x_pair)], axis=0)
            st = _dot(jnp.concatenate(b_scaled, axis=1), x_split)
            s_ref[pair] = s_old * jnp.where(lower_half[0:1, :], end_decay[0], end_decay[1]) + st
            if y_out is not None:
                y_pairs.append(jnp.where(lower_half, y_full[0], y_full[1])
                               + dskip_ref[:, ps] * xs_f32[:, ps])
        if y_out is not None:
            ws = slice(g * GROUP_WIDTH, (g + 1) * GROUP_WIDTH)
            y_g = jnp.concatenate(y_pairs, axis=1) * gate_c[:, ws]
            y_g = y_g * lax.rsqrt(jnp.mean(y_g * y_g, axis=-1, keepdims=True) + EPS) * ssmw_ref[:, ws]
            y_out(g, y_g.astype(BF16))


def _project_and_conv(hn, tm, win_ref, convw_ref, convb_ref,
                      xbuf, xs_f32, xs_bf, b_buf, c_buf, between=lambda: None):
    def project(cb):
        res = _dot(hn, win_ref[:, OFF_XBC + cb:OFF_XBC + cb + CONV_COL_BLOCK])
        for i in range(0, CONV_COL_BLOCK, LANES):
            xbuf[(cb + i) // LANES, CONV_TAIL:CONV_TAIL + tm, :] = res[:, i:i + LANES]

    project(0)
    for cb in range(0, D_XBC, CONV_COL_BLOCK):
        between()
        if cb + CONV_COL_BLOCK < D_XBC:
            project(cb + CONV_COL_BLOCK)
        for c0 in range(cb, cb + CONV_COL_BLOCK, LANES):
            cols = slice(c0, c0 + LANES)
            slab = c0 // LANES
            acc = convb_ref[:, cols] + convw_ref[CONV_WIDTH - 1:CONV_WIDTH, cols] * xbuf[slab, CONV_TAIL:CONV_TAIL + tm, :]
            for k in range(CONV_WIDTH - 1):
                start = CONV_TAIL - (CONV_WIDTH - 1) + k
                acc = acc + convw_ref[k:k + 1, cols] * xbuf[slab, start:start + tm, :]
            v = _silu(acc)
            if c0 < D_SSM:
                xs_f32[:, cols] = v
                xs_bf[:, cols] = v.astype(BF16)
            elif c0 < D_SSM + N_GROUPS * D_STATE:
                b_buf[:, c0 - D_SSM:c0 - D_SSM + LANES] = v
            else:
                off = c0 - D_SSM - N_GROUPS * D_STATE
                c_buf[:, off:off + LANES] = v
    tail = xbuf[:, tm:tm + CONV_TAIL, :]
    xbuf[:, 0:CONV_TAIL, :] = tail


def _dt_and_a(hn, win_ref, dtb_ref, alog_ref):
    dt = _softplus(_dot(hn, win_ref[:, OFF_DT:D_IN_PAD]) + dtb_ref[...])
    return dt, -jnp.exp(alog_ref[...])


def _meta_kernel(x_ref, nw_ref, win_ref, convw_ref, convb_ref, dtb_ref, alog_ref,
                 s_out, xtail_out, utail_out,
                 xbuf, xs_f32, xs_bf, b_buf, c_buf):
    s_out[...] = jnp.zeros_like(s_out)
    xbuf[:, 0:CONV_TAIL, :] = jnp.zeros((D_XBC // LANES, CONV_TAIL, LANES), F32)
    hn = _rms(x_ref[...], nw_ref[...]).astype(BF16)
    u = _dot(hn, win_ref[:, 0:D_POOL])
    for gi in range(len(POOL_WINDOWS)):
        utail_out[gi] = u[CHUNK - MAX_WINDOW:CHUNK, gi * POOL_GROUP:(gi + 1) * POOL_GROUP]
    _project_and_conv(hn, CHUNK, win_ref, convw_ref, convb_ref, xbuf, xs_f32, xs_bf, b_buf, c_buf)
    xtail_out[...] = xbuf[:, 0:CONV_TAIL, :]
    dt, a_row = _dt_and_a(hn, win_ref, dtb_ref, alog_ref)
    valid = lax.broadcasted_iota(jnp.int32, (CHUNK, DT_COLS), 0) >= CHUNK - N_META
    dt = jnp.where(valid, dt, 0.0)
    _ssd_chunk(_ssd_decays(dt, a_row), None, xs_bf[...], b_buf[...], c_buf[...], None, None, None, s_out, None)


def _layer_kernel(x_ref, nw_ref, win_ref, poolw_ref, pscale_ref, convw_ref, convb_ref, dtb_ref, alog_ref,
                  dskip_ref, ssmw_ref, wout_ref, s0_ref, xtail0_ref, utail0_ref,
                  nw_ffn_ref, w1_ref, w2_ref, nfw_ref,
                  o_ref,
                  s_ref, ubuf, xbuf, xs_f32, xs_bf, b_buf, c_buf, gate_buf, y_buf, h1_buf, h2_buf,
                  *, tiles_per_seq):
    tm = TM_MIX
    n = pl.program_id(0)

    @pl.when(n == 0)
    def _():
        h1_buf[...] = jnp.zeros_like(h1_buf)

    @pl.when(n % tiles_per_seq == 0)
    def _():
        s_ref[...] = s0_ref[...]
        ubuf[:, 0:MAX_WINDOW, :] = utail0_ref[...]
        xbuf[:, 0:CONV_TAIL, :] = xtail0_ref[...]

    h_prev = h1_buf[...]
    hn_prev = _rms(h_prev, nw_ffn_ref[...]).astype(BF16)
    h2_buf[...] = h_prev
    ff_blocks = iter(range(0, D_FF, FF_BLOCK))
    points = iter(range(len(MLP_POINTS)))

    def mlp_block():
        for _ in range(MLP_POINTS[next(points)]):
            j = next(ff_blocks)
            act = jnp.square(jnp.maximum(_dot(hn_prev, w1_ref[j // FF_BLOCK]), 0.0)).astype(BF16)
            h2_buf[...] += _dot_split_n(act, w2_ref, slice(j, j + FF_BLOCK))

    mlp_block()

    x = x_ref[0]
    hn = _rms(x, nw_ref[...]).astype(BF16)

    dt, a_row = _dt_and_a(hn, win_ref, dtb_ref, alog_ref)
    decays = [_ssd_decays(dt[c * CHUNK:(c + 1) * CHUNK, :], a_row) for c in range(tm // CHUNK)]

    u = _dot(hn, win_ref[:, 0:D_POOL])
    for gi, w in enumerate(POOL_WINDOWS):
        cols = slice(gi * POOL_GROUP, (gi + 1) * POOL_GROUP)
        ubuf[gi, MAX_WINDOW:MAX_WINDOW + tm, :] = u[:, cols]
        cur = ubuf[gi, MAX_WINDOW:MAX_WINDOW + tm, :]
        acc = cur
        for j in range(1, w):
            acc = acc + ubuf[gi, MAX_WINDOW - j:MAX_WINDOW - j + tm, :]
        pooled = acc * (1.0 / w) - cur
        mixed = _dot(pooled.astype(BF16), poolw_ref[gi])
        y_buf[:, cols] = (mixed * pscale_ref[:, cols]).astype(BF16)
    utail = ubuf[:, tm:tm + MAX_WINDOW, :]
    ubuf[:, 0:MAX_WINDOW, :] = utail

    for j in range(0, D_SSM, GATE_COL_BLOCK):
        gate_buf[:, j:j + GATE_COL_BLOCK] = _silu(_dot(hn, win_ref[:, OFF_Z + j:OFF_Z + j + GATE_COL_BLOCK]))
    _project_and_conv(hn, tm, win_ref, convw_ref, convb_ref, xbuf, xs_f32, xs_bf, b_buf, c_buf,
                      between=mlp_block)

    for c in range(tm // CHUNK):
        rows = slice(c * CHUNK, (c + 1) * CHUNK)

        def store_y(g, val, rows=rows):
            y_buf[rows, D_POOL + g * GROUP_WIDTH:D_POOL + (g + 1) * GROUP_WIDTH] = val

        _ssd_chunk(decays[c], xs_f32[rows, :], xs_bf[rows, :], b_buf[rows, :], c_buf[rows, :],
                   gate_buf[rows, :], dskip_ref, ssmw_ref, s_ref, store_y, between=mlp_block)
        h1_buf[rows, :] = x[rows, :] + _dot_split_n(y_buf[rows, :], wout_ref, slice(None))

    assert next(ff_blocks, None) is None and next(points, None) is None
    o_ref[0] = _rms(h2_buf[...], nfw_ref[...])


def _const_spec(shape):
    nd = len(shape)
    return pl.BlockSpec(shape, lambda *_: (0,) * nd, pipeline_mode=pl.Buffered(1))


def kernel(x, meta, norm_mix_w, w_in, pool_w, pool_scale, conv_w, conv_b, dt_bias, a_log, d_skip,
           ssm_norm_w, w_out, norm_ffn_w, w_ff1, w_ff2, norm_f_w):
    bsz, seq, d_model = x.shape
    assert d_model == D_MODEL and seq % TM_MIX == 0
    assert w_in.shape == (1, D_MODEL, OFF_DT + N_HEADS) and meta.shape == (N_META, D_MODEL)

    win = jnp.concatenate([w_in[0], jnp.zeros((D_MODEL, DT_COLS - N_HEADS), F32)], axis=1).astype(BF16)
    pad_heads = lambda v: jnp.concatenate([v, jnp.zeros((DT_COLS - N_HEADS,), F32)])[None, :]
    dtb = pad_heads(dt_bias[0])
    alog = pad_heads(a_log[0])
    dskip = jnp.repeat(d_skip[0], HEAD_DIM)[None, :]
    row = lambda v: v[None, :]
    nw = row(norm_mix_w[0])
    convw = conv_w[0]
    convb = row(conv_b[0])
    meta_pad = jnp.concatenate([jnp.zeros((CHUNK - N_META, D_MODEL), F32), meta.astype(F32)], axis=0)

    def col_blocks(w):
        k, n = w.shape
        return w.reshape(k, n // W_COL_BLOCK, W_COL_BLOCK).transpose(1, 0, 2).astype(BF16)

    state_shape = (N_PAIRS, D_STATE, LANES)
    xtail_shape = (D_XBC // LANES, CONV_TAIL, LANES)
    utail_shape = (D_POOL // LANES, MAX_WINDOW, LANES)
    s0, xtail0, utail0 = pl.pallas_call(
        _meta_kernel,
        out_shape=(jax.ShapeDtypeStruct(state_shape, F32),
                   jax.ShapeDtypeStruct(xtail_shape, F32),
                   jax.ShapeDtypeStruct(utail_shape, F32)),
        scratch_shapes=[
            pltpu.VMEM((D_XBC // LANES, CHUNK + CONV_TAIL, LANES), F32),
            pltpu.VMEM((CHUNK, D_SSM), F32),
            pltpu.VMEM((CHUNK, D_SSM), BF16),
            pltpu.VMEM((CHUNK, N_GROUPS * D_STATE), F32),
            pltpu.VMEM((CHUNK, N_GROUPS * D_STATE), F32),
        ],
        compiler_params=pltpu.CompilerParams(vmem_limit_bytes=VMEM_LIMIT),
        name="meta_prologue",
    )(meta_pad, nw, win, convw, convb, dtb, alog)

    tm = TM_MIX
    tiles_per_seq = seq // tm
    n_tiles = bsz * tiles_per_seq

    def x_index(n):
        m = jnp.minimum(n, n_tiles - 1)
        return (m // tiles_per_seq, m % tiles_per_seq, 0)

    def out_index(n):
        m = jnp.maximum(n - 1, 0)
        return (m // tiles_per_seq, m % tiles_per_seq, 0)

    return pl.pallas_call(
        functools.partial(_layer_kernel, tiles_per_seq=tiles_per_seq),
        out_shape=jax.ShapeDtypeStruct((bsz, seq, D_MODEL), F32),
        grid=(n_tiles + 1,),
        in_specs=[
            pl.BlockSpec((1, tm, D_MODEL), x_index),
            _const_spec((1, D_MODEL)),
            _const_spec((D_MODEL, D_IN_PAD)),
            _const_spec((N_GROUPS, POOL_GROUP, POOL_GROUP)),
            _const_spec((1, D_POOL)),
            _const_spec((CONV_WIDTH, D_XBC)),
            _const_spec((1, D_XBC)),
            _const_spec((1, DT_COLS)),
            _const_spec((1, DT_COLS)),
            _const_spec((1, D_SSM)),
            _const_spec((1, D_SSM)),
            _const_spec((D_MODEL // W_COL_BLOCK, D_MIX, W_COL_BLOCK)),
            _const_spec(state_shape),
            _const_spec(xtail_shape),
            _const_spec(utail_shape),
            _const_spec((1, D_MODEL)),
            _const_spec((D_FF // W_COL_BLOCK, D_MODEL, W_COL_BLOCK)),
            _const_spec((D_MODEL // W_COL_BLOCK, D_FF, W_COL_BLOCK)),
            _const_spec((1, D_MODEL)),
        ],
        out_specs=pl.BlockSpec((1, tm, D_MODEL), out_index),
        scratch_shapes=[
            pltpu.VMEM(state_shape, F32),
            pltpu.VMEM((D_POOL // LANES, tm + MAX_WINDOW, LANES), F32),
            pltpu.VMEM((D_XBC // LANES, tm + CONV_TAIL, LANES), F32),
            pltpu.VMEM((tm, D_SSM), F32),
            pltpu.VMEM((tm, D_SSM), BF16),
            pltpu.VMEM((tm, N_GROUPS * D_STATE), F32),
            pltpu.VMEM((tm, N_GROUPS * D_STATE), F32),
            pltpu.VMEM((tm, D_SSM), F32),
            pltpu.VMEM((tm, D_MIX), BF16),
            pltpu.VMEM((tm, D_MODEL), F32),
            pltpu.VMEM((tm, D_MODEL), F32),
        ],
        compiler_params=pltpu.CompilerParams(
            dimension_semantics=("arbitrary",), vmem_limit_bytes=VMEM_LIMIT),
        name="layer",
    )(x, nw, win, pool_w[0].astype(BF16), row(pool_scale[0]), convw, convb, dtb, alog, dskip,
      row(ssm_norm_w[0]), col_blocks(w_out[0]), s0, xtail0, utail0,
      row(norm_ffn_w[0]), col_blocks(w_ff1[0]), col_blocks(w_ff2[0]), row(norm_f_w))
```

```python
import functools

import jax
import jax.numpy as jnp
from jax import lax
from jax.experimental import pallas as pl
from jax.experimental.pallas import tpu as pltpu

D_MODEL = 1024
N_META = 16
POOL_WINDOWS = (2, 4, 8, 16)
D_POOL = 512
POOL_GROUP = 128
MAX_WINDOW = 16
D_SSM = 1536
HEAD_DIM = 64
N_HEADS = 24
N_GROUPS = 4
HEADS_PER_GROUP = 6
D_STATE = 128
CONV_WIDTH = 4
CHUNK = 128
D_XBC = 2560
D_MIX = 2048
D_FF = 4096
EPS = 1e-5

LANES = 128
SUBLANES = 8
DT_COLS = LANES
D_IN_PAD = D_POOL + D_SSM + D_XBC + DT_COLS
OFF_Z = D_POOL
OFF_XBC = D_POOL + D_SSM
OFF_DT = OFF_XBC + D_XBC
N_PAIRS = N_HEADS // 2
PAIRS_PER_GROUP = HEADS_PER_GROUP // 2
GROUP_WIDTH = HEADS_PER_GROUP * HEAD_DIM
CONV_TAIL = SUBLANES
CONV_COL_BLOCK = 512
GATE_COL_BLOCK = 512
X_SLABS = D_XBC // LANES
U_SLABS = D_POOL // LANES

TM_MIX = 256
W_COL_BLOCK = 512
FF_BLOCK = W_COL_BLOCK
VMEM_LIMIT = 60 * 1024 * 1024

F32 = jnp.float32
BF16 = jnp.bfloat16
LOG2E = 1.4426950408889634


def _dot(a, b):
    return jnp.dot(a, b, preferred_element_type=F32)


def _dot_split_n(a, w_ref, rows):
    return jnp.concatenate([_dot(a, w_ref[i, rows, :]) for i in range(w_ref.shape[0])], axis=1)


def _rms(x, w):
    return x * lax.rsqrt(jnp.mean(x * x, axis=-1, keepdims=True) + EPS) * w


def _silu(x):
    return x * (1.0 / (1.0 + jnp.exp(-x)))


def _softplus(x):
    return jnp.maximum(x, 0.0) + jnp.log(1.0 + jnp.exp(-jnp.abs(x)))


def _split3(x):
    hi = x.astype(BF16)
    r1 = x - hi.astype(F32)
    mid = r1.astype(BF16)
    lo = (r1 - mid.astype(F32)).astype(BF16)
    return hi, mid, lo


def _bcast_rows(row):
    return jnp.broadcast_to(row, (CHUNK, LANES))


def _causal_mask():
    row_id = lax.broadcasted_iota(jnp.int32, (CHUNK, CHUNK), 0)
    col_id = lax.broadcasted_iota(jnp.int32, (CHUNK, CHUNK), 1)
    return row_id >= col_id


def _ssd_decays(dt_c, a_row):
    ltri = jnp.where(_causal_mask(), 1.0, 0.0).astype(BF16)
    hi, mid, lo = _split3(dt_c * (a_row * LOG2E))
    a_cs = _dot(ltri, hi) + _dot(ltri, mid) + _dot(ltri, lo)
    a_cs_t = a_cs.T
    src_t = a_cs_t - jnp.log2(dt_c.T)
    to_end_t = jnp.exp2(a_cs_t[:, CHUNK - 1:CHUNK] - src_t)
    return a_cs, src_t, to_end_t


def _ssd_chunk(decays, xs_f32, xs_bf, b_c, c_c, z_c, dskip_ref, ssmw_ref, s_ref, y_out,
               between=lambda: None):
    a_cs, src_t, to_end_t = decays
    causal = _causal_mask()
    lower_half = lax.broadcasted_iota(jnp.int32, (CHUNK, LANES), 1) < HEAD_DIM

    for g in range(N_GROUPS):
        gs = slice(g * D_STATE, (g + 1) * D_STATE)
        b_g = b_c[:, gs]
        c_g = c_c[:, gs]
        b_t = b_g.T
        if y_out is not None:
            cb = lax.dot_general(c_g.astype(BF16), b_g.astype(BF16),
                                 (((1,), (1,)), ((), ())), preferred_element_type=F32)
        y_pairs = []
        for q in range(PAIRS_PER_GROUP):
            between()
            pair = g * PAIRS_PER_GROUP + q
            ps = slice(pair * LANES, (pair + 1) * LANES)
            x_pair = xs_bf[:, ps]
            s_old = s_ref[pair]
            rhs = jnp.concatenate([x_pair, s_old.astype(BF16)], axis=0)
            b_scaled, lhs, end_decay = [], [], []
            for half in range(2):
                h = 2 * pair + half
                col_b = jnp.broadcast_to(a_cs[:, h:h + 1], (CHUNK, LANES))
                end_decay.append(jnp.exp2(col_b[CHUNK - 1:CHUNK, :]))
                b_scaled.append((b_t * _bcast_rows(to_end_t[h:h + 1, :])).astype(BF16))
                if y_out is not None:
                    seg = jnp.where(causal, col_b - _bcast_rows(src_t[h:h + 1, :]), -jnp.inf)
                    m_h = cb * jnp.exp2(seg)
                    c_h = c_g * jnp.exp2(col_b)
                    lhs.append(jnp.concatenate([m_h.astype(BF16), c_h.astype(BF16)], axis=1))
            if y_out is not None:
                y_both = _dot(jnp.concatenate(lhs, axis=0), rhs)
                y_full = (y_both[0:CHUNK, :], y_both[CHUNK:2 * CHUNK, :])
            zeros = jnp.zeros_like(x_pair)
            x_split = jnp.concatenate([jnp.where(lower_half, x_pair, zeros),
                                       jnp.where(lower_half, zeros, x_pair)], axis=0)
            st = _dot(jnp.concatenate(b_scaled, axis=1), x_split)
            s_ref[pair] = s_old * jnp.where(lower_half[0:1, :], end_decay[0], end_decay[1]) + st
            if y_out is not None:
                y_pairs.append(jnp.where(lower_half, y_full[0], y_full[1])
                               + dskip_ref[:, ps] * xs_f32[:, ps])
        if y_out is not None:
            ws = slice(g * GROUP_WIDTH, (g + 1) * GROUP_WIDTH)
            y_g = jnp.concatenate(y_pairs, axis=1) * _silu(z_c[:, ws])
            y_g = y_g * lax.rsqrt(jnp.mean(y_g * y_g, axis=-1, keepdims=True) + EPS) * ssmw_ref[:, ws]
            y_out(g, y_g.astype(BF16))


def _project_xbc(hn, tm, win_ref, xbuf, base, cb):
    res = _dot(hn, win_ref[:, OFF_XBC + cb:OFF_XBC + cb + CONV_COL_BLOCK])
    for i in range(0, CONV_COL_BLOCK, LANES):
        xbuf[base + (cb + i) // LANES, CONV_TAIL:CONV_TAIL + tm, :] = res[:, i:i + LANES]


def _conv_block(tm, convw_ref, convb_ref, xbuf, base, cb, xs_f32, xs_bf, b_buf, c_buf):
    for c0 in range(cb, cb + CONV_COL_BLOCK, LANES):
        cols = slice(c0, c0 + LANES)
        slab = base + c0 // LANES
        acc = convb_ref[:, cols] + convw_ref[CONV_WIDTH - 1:CONV_WIDTH, cols] * xbuf[slab, CONV_TAIL:CONV_TAIL + tm, :]
        for k in range(CONV_WIDTH - 1):
            start = CONV_TAIL - (CONV_WIDTH - 1) + k
            acc = acc + convw_ref[k:k + 1, cols] * xbuf[slab, start:start + tm, :]
        v = _silu(acc)
        if c0 < D_SSM:
            xs_f32[:, cols] = v
            xs_bf[:, cols] = v.astype(BF16)
        elif c0 < D_SSM + N_GROUPS * D_STATE:
            b_buf[:, c0 - D_SSM:c0 - D_SSM + LANES] = v
        else:
            off = c0 - D_SSM - N_GROUPS * D_STATE
            c_buf[:, off:off + LANES] = v


def _dt_and_a(dt_raw, dtb_ref, alog_ref):
    return _softplus(dt_raw + dtb_ref[...]), -jnp.exp(alog_ref[...])


def _meta_kernel(x_ref, nw_ref, win_ref, convw_ref, convb_ref, dtb_ref, alog_ref,
                 s_out, xtail_out, utail_out,
                 xbuf, xs_f32, xs_bf, b_buf, c_buf):
    s_out[...] = jnp.zeros_like(s_out)
    xbuf[:, 0:CONV_TAIL, :] = jnp.zeros((X_SLABS, CONV_TAIL, LANES), F32)
    hn = _rms(x_ref[...], nw_ref[...]).astype(BF16)
    u = _dot(hn, win_ref[:, 0:D_POOL])
    for gi in range(U_SLABS):
        utail_out[gi] = u[CHUNK - MAX_WINDOW:CHUNK, gi * POOL_GROUP:(gi + 1) * POOL_GROUP]
    for cb in range(0, D_XBC, CONV_COL_BLOCK):
        _project_xbc(hn, CHUNK, win_ref, xbuf, 0, cb)
        _conv_block(CHUNK, convw_ref, convb_ref, xbuf, 0, cb, xs_f32, xs_bf, b_buf, c_buf)
    xtail_out[...] = xbuf[:, CHUNK:CHUNK + CONV_TAIL, :]
    dt, a_row = _dt_and_a(_dot(hn, win_ref[:, OFF_DT:D_IN_PAD]), dtb_ref, alog_ref)
    valid = lax.broadcasted_iota(jnp.int32, (CHUNK, DT_COLS), 0) >= CHUNK - N_META
    dt = jnp.where(valid, dt, 0.0)
    _ssd_chunk(_ssd_decays(dt, a_row), None, xs_bf[...], b_buf[...], c_buf[...], None, None, None, s_out, None)


def _spread(n_items, n_slots):
    return [(i + 1) * n_items // n_slots - i * n_items // n_slots for i in range(n_slots)]


def _layer_kernel(x_ref, nw_ref, win_ref, poolw_ref, pscale_ref, convw_ref, convb_ref, dtb_ref,
                  alog_ref, dskip_ref, ssmw_ref, wout_ref, s0_ref, xtail0_ref, utail0_ref,
                  nw_ffn_ref, w1_ref, w2_ref, nfw_ref,
                  o_ref,
                  s_ref, ubuf, xbuf, zbuf, xs_f32, xs_bf, b_buf, c_buf, y_buf, h1_buf, h2_buf,
                  *, tiles_per_seq):
    tm = TM_MIX
    n = pl.program_id(0)

    @pl.when(n == 0)
    def _():
        h1_buf[...] = jnp.zeros_like(h1_buf)

    @pl.when(n % tiles_per_seq == 0)
    def _():
        s_ref[...] = s0_ref[...]
        ubuf[:, 0:MAX_WINDOW, :] = utail0_ref[...]
        xbuf[:, 0:CONV_TAIL, :] = xtail0_ref[...]

    h_prev = h1_buf[...]
    hn_prev = _rms(h_prev, nw_ffn_ref[...]).astype(BF16)
    h2_buf[...] = h_prev
    x = x_ref[0]
    hn = _rms(x, nw_ref[...]).astype(BF16)
    acts = {}

    def mlp_up(j):
        acts[j] = jnp.square(jnp.maximum(_dot(hn_prev, w1_ref[j]), 0.0)).astype(BF16)

    def mlp_down(j):
        h2_buf[...] += _dot_split_n(acts.pop(j), w2_ref, slice(j * FF_BLOCK, (j + 1) * FF_BLOCK))

    n_ff = D_FF // FF_BLOCK
    fillers = [functools.partial(mlp_up, 0)]
    for j in range(1, n_ff):
        fillers += [functools.partial(mlp_up, j), functools.partial(mlp_down, j - 1)]
    fillers.append(functools.partial(mlp_down, n_ff - 1))
    fillers.pop(0)()
    quota = iter(_spread(len(fillers), (tm // CHUNK) * N_PAIRS))

    def fill():
        for _ in range(next(quota)):
            fillers.pop(0)()

    dt, a_row = _dt_and_a(_dot(hn, win_ref[:, OFF_DT:D_IN_PAD]), dtb_ref, alog_ref)
    for j in range(0, D_SSM, GATE_COL_BLOCK):
        zbuf[:, j:j + GATE_COL_BLOCK] = _dot(hn, win_ref[:, OFF_Z + j:OFF_Z + j + GATE_COL_BLOCK])
    u = _dot(hn, win_ref[:, 0:D_POOL])
    decays = [_ssd_decays(dt[c * CHUNK:(c + 1) * CHUNK, :], a_row) for c in range(tm // CHUNK)]

    for gi, w in enumerate(POOL_WINDOWS):
        cols = slice(gi * POOL_GROUP, (gi + 1) * POOL_GROUP)
        ubuf[gi, MAX_WINDOW:MAX_WINDOW + tm, :] = u[:, cols]
        tok = ubuf[gi, MAX_WINDOW:MAX_WINDOW + tm, :]
        acc = tok
        for j in range(1, w):
            acc = acc + ubuf[gi, MAX_WINDOW - j:MAX_WINDOW - j + tm, :]
        pooled = acc * (1.0 / w) - tok
        mixed = _dot(pooled.astype(BF16), poolw_ref[gi])
        y_buf[:, cols] = (mixed * pscale_ref[:, cols]).astype(BF16)
    utail = ubuf[:, tm:tm + MAX_WINDOW, :]
    ubuf[:, 0:MAX_WINDOW, :] = utail

    _project_xbc(hn, tm, win_ref, xbuf, 0, 0)
    for cb in range(0, D_XBC, CONV_COL_BLOCK):
        if cb + CONV_COL_BLOCK < D_XBC:
            _project_xbc(hn, tm, win_ref, xbuf, 0, cb + CONV_COL_BLOCK)
        _conv_block(tm, convw_ref, convb_ref, xbuf, 0, cb, xs_f32, xs_bf, b_buf, c_buf)
    xtail = xbuf[:, tm:tm + CONV_TAIL, :]
    xbuf[:, 0:CONV_TAIL, :] = xtail

    for c in range(tm // CHUNK):
        rows = slice(c * CHUNK, (c + 1) * CHUNK)

        def store_y(g, val, rows=rows):
            y_buf[rows, D_POOL + g * GROUP_WIDTH:D_POOL + (g + 1) * GROUP_WIDTH] = val

        _ssd_chunk(decays[c], xs_f32[rows, :], xs_bf[rows, :], b_buf[rows, :], c_buf[rows, :],
                   zbuf[rows, :], dskip_ref, ssmw_ref, s_ref, store_y, between=fill)
        h1_buf[rows, :] = x[rows, :] + _dot_split_n(y_buf[rows, :], wout_ref, slice(None))

    assert not fillers and not acts and next(quota, None) is None
    o_ref[0] = _rms(h2_buf[...], nfw_ref[...])


def _const_spec(shape):
    nd = len(shape)
    return pl.BlockSpec(shape, lambda *_: (0,) * nd, pipeline_mode=pl.Buffered(1))


def kernel(x, meta, norm_mix_w, w_in, pool_w, pool_scale, conv_w, conv_b, dt_bias, a_log, d_skip,
           ssm_norm_w, w_out, norm_ffn_w, w_ff1, w_ff2, norm_f_w):
    bsz, seq, d_model = x.shape
    assert d_model == D_MODEL and seq % TM_MIX == 0
    assert w_in.shape == (1, D_MODEL, OFF_DT + N_HEADS) and meta.shape == (N_META, D_MODEL)

    win = jnp.concatenate([w_in[0], jnp.zeros((D_MODEL, DT_COLS - N_HEADS), F32)], axis=1).astype(BF16)
    pad_heads = lambda v: jnp.concatenate([v, jnp.zeros((DT_COLS - N_HEADS,), F32)])[None, :]
    dtb = pad_heads(dt_bias[0])
    alog = pad_heads(a_log[0])
    dskip = jnp.repeat(d_skip[0], HEAD_DIM)[None, :]
    row = lambda v: v[None, :]
    nw = row(norm_mix_w[0])
    convw = conv_w[0]
    convb = row(conv_b[0])
    meta_pad = jnp.concatenate([jnp.zeros((CHUNK - N_META, D_MODEL), F32), meta.astype(F32)], axis=0)

    def col_blocks(w):
        k, n = w.shape
        return w.reshape(k, n // W_COL_BLOCK, W_COL_BLOCK).transpose(1, 0, 2).astype(BF16)

    state_shape = (N_PAIRS, D_STATE, LANES)
    xtail_shape = (X_SLABS, CONV_TAIL, LANES)
    utail_shape = (U_SLABS, MAX_WINDOW, LANES)
    s0, xtail0, utail0 = pl.pallas_call(
        _meta_kernel,
        out_shape=(jax.ShapeDtypeStruct(state_shape, F32),
                   jax.ShapeDtypeStruct(xtail_shape, F32),
                   jax.ShapeDtypeStruct(utail_shape, F32)),
        scratch_shapes=[
            pltpu.VMEM((X_SLABS, CHUNK + CONV_TAIL, LANES), F32),
            pltpu.VMEM((CHUNK, D_SSM), F32),
            pltpu.VMEM((CHUNK, D_SSM), BF16),
            pltpu.VMEM((CHUNK, N_GROUPS * D_STATE), F32),
            pltpu.VMEM((CHUNK, N_GROUPS * D_STATE), F32),
        ],
        compiler_params=pltpu.CompilerParams(vmem_limit_bytes=VMEM_LIMIT),
        name="meta_prologue",
    )(meta_pad, nw, win, convw, convb, dtb, alog)

    tm = TM_MIX
    tiles_per_seq = seq // tm
    n_tiles = bsz * tiles_per_seq

    def tile_index(m):
        m = jnp.clip(m, 0, n_tiles - 1)
        return (m // tiles_per_seq, m % tiles_per_seq, 0)

    return pl.pallas_call(
        functools.partial(_layer_kernel, tiles_per_seq=tiles_per_seq),
        out_shape=jax.ShapeDtypeStruct((bsz, seq, D_MODEL), F32),
        grid=(n_tiles + 1,),
        in_specs=[
            pl.BlockSpec((1, tm, D_MODEL), lambda n: tile_index(n)),
            _const_spec((1, D_MODEL)),
            _const_spec((D_MODEL, D_IN_PAD)),
            _const_spec((N_GROUPS, POOL_GROUP, POOL_GROUP)),
            _const_spec((1, D_POOL)),
            _const_spec((CONV_WIDTH, D_XBC)),
            _const_spec((1, D_XBC)),
            _const_spec((1, DT_COLS)),
            _const_spec((1, DT_COLS)),
            _const_spec((1, D_SSM)),
            _const_spec((1, D_SSM)),
            _const_spec((D_MODEL // W_COL_BLOCK, D_MIX, W_COL_BLOCK)),
            _const_spec(state_shape),
            _const_spec(xtail_shape),
            _const_spec(utail_shape),
            _const_spec((1, D_MODEL)),
            _const_spec((D_FF // W_COL_BLOCK, D_MODEL, W_COL_BLOCK)),
            _const_spec((D_MODEL // W_COL_BLOCK, D_FF, W_COL_BLOCK)),
            _const_spec((1, D_MODEL)),
        ],
        out_specs=pl.BlockSpec((1, tm, D_MODEL), lambda n: tile_index(n - 1)),
        scratch_shapes=[
            pltpu.VMEM(state_shape, F32),
            pltpu.VMEM((U_SLABS, tm + MAX_WINDOW, LANES), F32),
            pltpu.VMEM((X_SLABS, tm + CONV_TAIL, LANES), F32),
            pltpu.VMEM((tm, D_SSM), F32),
            pltpu.VMEM((tm, D_SSM), F32),
            pltpu.VMEM((tm, D_SSM), BF16),
            pltpu.VMEM((tm, N_GROUPS * D_STATE), F32),
            pltpu.VMEM((tm, N_GROUPS * D_STATE), F32),
            pltpu.VMEM((tm, D_MIX), BF16),
            pltpu.VMEM((tm, D_MODEL), F32),
            pltpu.VMEM((tm, D_MODEL), F32),
        ],
        compiler_params=pltpu.CompilerParams(
            dimension_semantics=("arbitrary",), vmem_limit_bytes=VMEM_LIMIT),
        name="layer",
    )(x, nw, win, pool_w[0].astype(BF16), row(pool_scale[0]), convw, convb, dtb, alog, dskip,
      row(ssm_norm_w[0]), col_blocks(w_out[0]), s0, xtail0, utail0,
      row(norm_ffn_w[0]), col_blocks(w_ff1[0]), col_blocks(w_ff2[0]), row(norm_f_w))
```

```python
import functools

import jax
import jax.numpy as jnp
from jax import lax
from jax.experimental import pallas as pl
from jax.experimental.pallas import tpu as pltpu

D_MODEL = 1024
N_META = 16
POOL_WINDOWS = (2, 4, 8, 16)
D_POOL = 512
POOL_GROUP = 128
MAX_WINDOW = 16
D_SSM = 1536
HEAD_DIM = 64
N_HEADS = 24
N_GROUPS = 4
HEADS_PER_GROUP = 6
D_STATE = 128
CONV_WIDTH = 4
CHUNK = 128
D_XBC = 2560
D_MIX = 2048
D_FF = 4096
EPS = 1e-5

LANES = 128
SUBLANES = 8
DT_COLS = LANES
D_IN_PAD = D_POOL + D_SSM + D_XBC + DT_COLS
OFF_Z = D_POOL
OFF_XBC = D_POOL + D_SSM
OFF_DT = OFF_XBC + D_XBC
N_PAIRS = N_HEADS // 2
PAIRS_PER_GROUP = HEADS_PER_GROUP // 2
GROUP_WIDTH = HEADS_PER_GROUP * HEAD_DIM
CONV_TAIL = SUBLANES
CONV_COL_BLOCK = 512
GATE_COL_BLOCK = 512
X_SLABS = D_XBC // LANES
U_SLABS = D_POOL // LANES

TM_MIX = 256
W_COL_BLOCK = 512
FF_BLOCK = W_COL_BLOCK
VMEM_LIMIT = 60 * 1024 * 1024

F32 = jnp.float32
BF16 = jnp.bfloat16
LOG2E = 1.4426950408889634


def _dot(a, b):
    return jnp.dot(a, b, preferred_element_type=F32)


def _dot_split_n(a, w_ref, rows):
    return jnp.concatenate([_dot(a, w_ref[i, rows, :]) for i in range(w_ref.shape[0])], axis=1)


def _rms(x, w):
    return x * lax.rsqrt(jnp.mean(x * x, axis=-1, keepdims=True) + EPS) * w


def _silu(x):
    return x * (1.0 / (1.0 + jnp.exp(-x)))


def _softplus(x):
    return jnp.maximum(x, 0.0) + jnp.log(1.0 + jnp.exp(-jnp.abs(x)))


def _split3(x):
    hi = x.astype(BF16)
    r1 = x - hi.astype(F32)
    mid = r1.astype(BF16)
    lo = (r1 - mid.astype(F32)).astype(BF16)
    return hi, mid, lo


def _bcast_rows(row):
    return jnp.broadcast_to(row, (CHUNK, LANES))


def _causal_mask():
    row_id = lax.broadcasted_iota(jnp.int32, (CHUNK, CHUNK), 0)
    col_id = lax.broadcasted_iota(jnp.int32, (CHUNK, CHUNK), 1)
    return row_id >= col_id


def _ssd_decays(dt_c, a_row):
    ltri = jnp.where(_causal_mask(), 1.0, 0.0).astype(BF16)
    hi, mid, lo = _split3(dt_c * (a_row * LOG2E))
    a_cs = _dot(ltri, hi) + _dot(ltri, mid) + _dot(ltri, lo)
    a_cs_t = a_cs.T
    src_t = a_cs_t - jnp.log2(dt_c.T)
    to_end_t = jnp.exp2(a_cs_t[:, CHUNK - 1:CHUNK] - src_t)
    return a_cs, src_t, to_end_t


def _ssd_chunk(decays, xs_f32, xs_bf, b_c, c_c, z_c, dskip_ref, ssmw_ref, s_ref, y_out,
               between=lambda: None):
    a_cs, src_t, to_end_t = decays
    causal = _causal_mask()
    lower_half = lax.broadcasted_iota(jnp.int32, (CHUNK, LANES), 1) < HEAD_DIM

    for g in range(N_GROUPS):
        gs = slice(g * D_STATE, (g + 1) * D_STATE)
        b_g = b_c[:, gs]
        c_g = c_c[:, gs]
        b_t = b_g.T
        if y_out is not None:
            cb = lax.dot_general(c_g.astype(BF16), b_g.astype(BF16),
                                 (((1,), (1,)), ((), ())), preferred_element_type=F32)
        y_pairs = []
        for q in range(PAIRS_PER_GROUP):
            between()
            pair = g * PAIRS_PER_GROUP + q
            ps = slice(pair * LANES, (pair + 1) * LANES)
            x_pair = xs_bf[:, ps]
            s_old = s_ref[pair]
            rhs = jnp.concatenate([x_pair, s_old.astype(BF16)], axis=0)
            b_scaled, lhs, end_decay = [], [], []
            for half in range(2):
                h = 2 * pair + half
                col_b = jnp.broadcast_to(a_cs[:, h:h + 1], (CHUNK, LANES))
                end_decay.append(jnp.exp2(col_b[CHUNK - 1:CHUNK, :]))
                b_scaled.append((b_t * _bcast_rows(to_end_t[h:h + 1, :])).astype(BF16))
                if y_out is not None:
                    seg = jnp.where(causal, col_b - _bcast_rows(src_t[h:h + 1, :]), -jnp.inf)
                    m_h = cb * jnp.exp2(seg)
                    c_h = c_g * jnp.exp2(col_b)
                    lhs.append(jnp.concatenate([m_h.astype(BF16), c_h.astype(BF16)], axis=1))
            if y_out is not None:
                y_both = _dot(jnp.concatenate(lhs, axis=0), rhs)
                y_full = (y_both[0:CHUNK, :], y_both[CHUNK:2 * CHUNK, :])
            zeros = jnp.zeros_like(x_pair)
            x_split = jnp.concatenate([jnp.where(lower_half, x_pair, zeros),
                                       jnp.where(lower_half, zeros, x_pair)], axis=0)
            st = _dot(jnp.concatenate(b_scaled, axis=1), x_split)
            s_ref[pair] = s_old * jnp.where(lower_half[0:1, :], end_decay[0], end_decay[1]) + st
            if y_out is not None:
                y_pairs.append(jnp.where(lower_half, y_full[0], y_full[1])
                               + dskip_ref[:, ps] * xs_f32[:, ps])
        if y_out is not None:
            ws = slice(g * GROUP_WIDTH, (g + 1) * GROUP_WIDTH)
            y_g = jnp.concatenate(y_pairs, axis=1) * _silu(z_c[:, ws])
            y_g = y_g * lax.rsqrt(jnp.mean(y_g * y_g, axis=-1, keepdims=True) + EPS) * ssmw_ref[:, ws]
            y_out(g, y_g.astype(BF16))


def _project_xbc(hn, tm, win_ref, xbuf, base, cb):
    res = _dot(hn, win_ref[:, OFF_XBC + cb:OFF_XBC + cb + CONV_COL_BLOCK])
    for i in range(0, CONV_COL_BLOCK, LANES):
        xbuf[base + (cb + i) // LANES, CONV_TAIL:CONV_TAIL + tm, :] = res[:, i:i + LANES]


def _conv_block(tm, convw_ref, convb_ref, xbuf, base, cb, xs_f32, xs_bf, b_buf, c_buf):
    for c0 in range(cb, cb + CONV_COL_BLOCK, LANES):
        cols = slice(c0, c0 + LANES)
        slab = base + c0 // LANES
        acc = convb_ref[:, cols] + convw_ref[CONV_WIDTH - 1:CONV_WIDTH, cols] * xbuf[slab, CONV_TAIL:CONV_TAIL + tm, :]
        for k in range(CONV_WIDTH - 1):
            start = CONV_TAIL - (CONV_WIDTH - 1) + k
            acc = acc + convw_ref[k:k + 1, cols] * xbuf[slab, start:start + tm, :]
        v = _silu(acc)
        if c0 < D_SSM:
            xs_f32[:, cols] = v
            xs_bf[:, cols] = v.astype(BF16)
        elif c0 < D_SSM + N_GROUPS * D_STATE:
            b_buf[:, c0 - D_SSM:c0 - D_SSM + LANES] = v
        else:
            off = c0 - D_SSM - N_GROUPS * D_STATE
            c_buf[:, off:off + LANES] = v


def _dt_and_a(dt_raw, dtb_ref, alog_ref):
    return _softplus(dt_raw + dtb_ref[...]), -jnp.exp(alog_ref[...])


def _meta_kernel(x_ref, nw_ref, win_ref, convw_ref, convb_ref, dtb_ref, alog_ref,
                 s_out, xtail_out, utail_out,
                 xbuf, xs_f32, xs_bf, b_buf, c_buf):
    s_out[...] = jnp.zeros_like(s_out)
    xbuf[:, 0:CONV_TAIL, :] = jnp.zeros((X_SLABS, CONV_TAIL, LANES), F32)
    hn = _rms(x_ref[...], nw_ref[...]).astype(BF16)
    u = _dot(hn, win_ref[:, 0:D_POOL])
    for gi in range(U_SLABS):
        utail_out[gi] = u[CHUNK - MAX_WINDOW:CHUNK, gi * POOL_GROUP:(gi + 1) * POOL_GROUP]
    for cb in range(0, D_XBC, CONV_COL_BLOCK):
        _project_xbc(hn, CHUNK, win_ref, xbuf, 0, cb)
        _conv_block(CHUNK, convw_ref, convb_ref, xbuf, 0, cb, xs_f32, xs_bf, b_buf, c_buf)
    xtail_out[...] = xbuf[:, CHUNK:CHUNK + CONV_TAIL, :]
    dt, a_row = _dt_and_a(_dot(hn, win_ref[:, OFF_DT:D_IN_PAD]), dtb_ref, alog_ref)
    valid = lax.broadcasted_iota(jnp.int32, (CHUNK, DT_COLS), 0) >= CHUNK - N_META
    dt = jnp.where(valid, dt, 0.0)
    _ssd_chunk(_ssd_decays(dt, a_row), None, xs_bf[...], b_buf[...], c_buf[...], None, None, None, s_out, None)


def _spread(n_items, n_slots):
    return [-(-(i + 1) * n_items // n_slots) + (-i * n_items // n_slots) for i in range(n_slots)]


def _in_projection(hn, tm, win_ref, slot):
    ubuf, xbuf, zbuf, dtbuf = slot

    def dt_unit():
        dtbuf[...] = _dot(hn, win_ref[:, OFF_DT:D_IN_PAD])

    def pool_unit():
        u = _dot(hn, win_ref[:, 0:D_POOL])
        for gi in range(U_SLABS):
            ubuf[gi, MAX_WINDOW:MAX_WINDOW + tm, :] = u[:, gi * LANES:(gi + 1) * LANES]

    def gate_unit(j):
        zbuf[:, j:j + GATE_COL_BLOCK] = _dot(hn, win_ref[:, OFF_Z + j:OFF_Z + j + GATE_COL_BLOCK])

    units = [dt_unit, pool_unit]
    units += [functools.partial(_project_xbc, hn, tm, win_ref, xbuf, 0, cb)
              for cb in range(0, D_XBC, CONV_COL_BLOCK)]
    units += [functools.partial(gate_unit, j) for j in range(0, D_SSM, GATE_COL_BLOCK)]
    return units


def _layer_kernel(xn_ref, xc_ref, nw_ref, win_ref, poolw_ref, pscale_ref, convw_ref, convb_ref, dtb_ref,
                  alog_ref, dskip_ref, ssmw_ref, wout_ref, s0_ref, xtail0_ref, utail0_ref,
                  nw_ffn_ref, w1_ref, w2_ref, nfw_ref,
                  o_ref,
                  s_ref, ubuf0, xbuf0, zbuf0, dtbuf0, ubuf1, xbuf1, zbuf1, dtbuf1,
                  xs_f32, xs_bf, b_buf, c_buf, y_buf, h1_buf, h2_buf,
                  *, tiles_per_seq):
    n = pl.program_id(0)
    slots = ((ubuf0, xbuf0, zbuf0, dtbuf0), (ubuf1, xbuf1, zbuf1, dtbuf1))

    @pl.when(n == 0)
    def _():
        h1_buf[...] = jnp.zeros_like(h1_buf)
        ubuf1[...] = jnp.zeros_like(ubuf1)
        xbuf1[...] = jnp.zeros_like(xbuf1)
        hn0 = _rms(xc_ref[0], nw_ref[...]).astype(BF16)
        for unit in _in_projection(hn0, TM_MIX, win_ref, slots[0]):
            unit()

    first = n % tiles_per_seq == 0

    @pl.when(first)
    def _():
        s_ref[...] = s0_ref[...]

    args = (xn_ref, xc_ref, nw_ref, win_ref, poolw_ref, pscale_ref, convw_ref, convb_ref, dtb_ref,
            alog_ref, dskip_ref, ssmw_ref, wout_ref, xtail0_ref, utail0_ref,
            nw_ffn_ref, w1_ref, w2_ref, nfw_ref, o_ref,
            s_ref, xs_f32, xs_bf, b_buf, c_buf, y_buf, h1_buf, h2_buf)
    for parity in range(2):
        pl.when(n % 2 == parity)(functools.partial(_layer_step, first, slots[parity], slots[1 - parity], *args))


def _layer_step(first, cur, nxt, xn_ref, xc_ref, nw_ref, win_ref, poolw_ref, pscale_ref, convw_ref,
                convb_ref, dtb_ref, alog_ref, dskip_ref, ssmw_ref, wout_ref, xtail0_ref, utail0_ref,
                nw_ffn_ref, w1_ref, w2_ref, nfw_ref, o_ref,
                s_ref, xs_f32, xs_bf, b_buf, c_buf, y_buf, h1_buf, h2_buf):
    tm = TM_MIX
    ubuf, xbuf, zbuf, dtbuf = cur

    xbuf[:, 0:CONV_TAIL, :] = jnp.where(first, xtail0_ref[...], nxt[1][:, tm:tm + CONV_TAIL, :])
    ubuf[:, 0:MAX_WINDOW, :] = jnp.where(first, utail0_ref[...], nxt[0][:, tm:tm + MAX_WINDOW, :])

    h_prev = h1_buf[...]
    hn_prev = _rms(h_prev, nw_ffn_ref[...]).astype(BF16)
    h2_buf[...] = h_prev
    hn_next = _rms(xn_ref[0], nw_ref[...]).astype(BF16)
    x = xc_ref[0]
    acts = {}

    def mlp_up(j):
        acts[j] = jnp.square(jnp.maximum(_dot(hn_prev, w1_ref[j]), 0.0)).astype(BF16)

    def mlp_down(j):
        h2_buf[...] += _dot_split_n(acts.pop(j), w2_ref, slice(j * FF_BLOCK, (j + 1) * FF_BLOCK))

    n_ff = D_FF // FF_BLOCK
    mlp_units = [functools.partial(mlp_up, 0)]
    for j in range(1, n_ff):
        mlp_units += [functools.partial(mlp_up, j), functools.partial(mlp_down, j - 1)]
    mlp_units.append(functools.partial(mlp_down, n_ff - 1))
    proj_units = _in_projection(hn_next, tm, win_ref, nxt)
    fillers = []
    for i, k in enumerate(_spread(len(proj_units), len(mlp_units))):
        fillers.append(mlp_units[i])
        fillers += [proj_units.pop(0) for _ in range(k)]
    quota = iter(_spread(len(fillers), 2 + D_XBC // CONV_COL_BLOCK + (tm // CHUNK) * N_PAIRS))

    def fill():
        for _ in range(next(quota)):
            fillers.pop(0)()

    fill()
    dt, a_row = _dt_and_a(dtbuf[...], dtb_ref, alog_ref)
    decays = [_ssd_decays(dt[c * CHUNK:(c + 1) * CHUNK, :], a_row) for c in range(tm // CHUNK)]

    fill()
    for gi, w in enumerate(POOL_WINDOWS):
        cols = slice(gi * POOL_GROUP, (gi + 1) * POOL_GROUP)
        tok = ubuf[gi, MAX_WINDOW:MAX_WINDOW + tm, :]
        acc = tok
        for j in range(1, w):
            acc = acc + ubuf[gi, MAX_WINDOW - j:MAX_WINDOW - j + tm, :]
        pooled = acc * (1.0 / w) - tok
        mixed = _dot(pooled.astype(BF16), poolw_ref[gi])
        y_buf[:, cols] = (mixed * pscale_ref[:, cols]).astype(BF16)

    for cb in range(0, D_XBC, CONV_COL_BLOCK):
        fill()
        _conv_block(tm, convw_ref, convb_ref, xbuf, 0, cb, xs_f32, xs_bf, b_buf, c_buf)

    for c in range(tm // CHUNK):
        rows = slice(c * CHUNK, (c + 1) * CHUNK)

        def store_y(g, val, rows=rows):
            y_buf[rows, D_POOL + g * GROUP_WIDTH:D_POOL + (g + 1) * GROUP_WIDTH] = val

        _ssd_chunk(decays[c], xs_f32[rows, :], xs_bf[rows, :], b_buf[rows, :], c_buf[rows, :],
                   zbuf[rows, :], dskip_ref, ssmw_ref, s_ref, store_y, between=fill)
        h1_buf[rows, :] = x[rows, :] + _dot_split_n(y_buf[rows, :], wout_ref, slice(None))

    assert not fillers and not acts and next(quota, None) is None
    o_ref[0] = _rms(h2_buf[...], nfw_ref[...])


def _const_spec(shape):
    nd = len(shape)
    return pl.BlockSpec(shape, lambda *_: (0,) * nd, pipeline_mode=pl.Buffered(1))


def kernel(x, meta, norm_mix_w, w_in, pool_w, pool_scale, conv_w, conv_b, dt_bias, a_log, d_skip,
           ssm_norm_w, w_out, norm_ffn_w, w_ff1, w_ff2, norm_f_w):
    bsz, seq, d_model = x.shape
    assert d_model == D_MODEL and seq % TM_MIX == 0
    assert w_in.shape == (1, D_MODEL, OFF_DT + N_HEADS) and meta.shape == (N_META, D_MODEL)

    win = jnp.concatenate([w_in[0], jnp.zeros((D_MODEL, DT_COLS - N_HEADS), F32)], axis=1).astype(BF16)
    pad_heads = lambda v: jnp.concatenate([v, jnp.zeros((DT_COLS - N_HEADS,), F32)])[None, :]
    dtb = pad_heads(dt_bias[0])
    alog = pad_heads(a_log[0])
    dskip = jnp.repeat(d_skip[0], HEAD_DIM)[None, :]
    row = lambda v: v[None, :]
    nw = row(norm_mix_w[0])
    convw = conv_w[0]
    convb = row(conv_b[0])
    meta_pad = jnp.concatenate([jnp.zeros((CHUNK - N_META, D_MODEL), F32), meta.astype(F32)], axis=0)

    def col_blocks(w):
        k, n = w.shape
        return w.reshape(k, n // W_COL_BLOCK, W_COL_BLOCK).transpose(1, 0, 2).astype(BF16)

    state_shape = (N_PAIRS, D_STATE, LANES)
    xtail_shape = (X_SLABS, CONV_TAIL, LANES)
    utail_shape = (U_SLABS, MAX_WINDOW, LANES)
    s0, xtail0, utail0 = pl.pallas_call(
        _meta_kernel,
        out_shape=(jax.ShapeDtypeStruct(state_shape, F32),
                   jax.ShapeDtypeStruct(xtail_shape, F32),
                   jax.ShapeDtypeStruct(utail_shape, F32)),
        scratch_shapes=[
            pltpu.VMEM((X_SLABS, CHUNK + CONV_TAIL, LANES), F32),
            pltpu.VMEM((CHUNK, D_SSM), F32),
            pltpu.VMEM((CHUNK, D_SSM), BF16),
            pltpu.VMEM((CHUNK, N_GROUPS * D_STATE), F32),
            pltpu.VMEM((CHUNK, N_GROUPS * D_STATE), F32),
        ],
        compiler_params=pltpu.CompilerParams(vmem_limit_bytes=VMEM_LIMIT),
        name="meta_prologue",
    )(meta_pad, nw, win, convw, convb, dtb, alog)

    tm = TM_MIX
    tiles_per_seq = seq // tm
    n_tiles = bsz * tiles_per_seq

    def tile_index(m):
        m = jnp.clip(m, 0, n_tiles - 1)
        return (m // tiles_per_seq, m % tiles_per_seq, 0)

    return pl.pallas_call(
        functools.partial(_layer_kernel, tiles_per_seq=tiles_per_seq),
        out_shape=jax.ShapeDtypeStruct((bsz, seq, D_MODEL), F32),
        grid=(n_tiles + 1,),
        in_specs=[
            pl.BlockSpec((1, tm, D_MODEL), lambda n: tile_index(n + 1)),
            pl.BlockSpec((1, tm, D_MODEL), lambda n: tile_index(n)),
            _const_spec((1, D_MODEL)),
            _const_spec((D_MODEL, D_IN_PAD)),
            _const_spec((N_GROUPS, POOL_GROUP, POOL_GROUP)),
            _const_spec((1, D_POOL)),
            _const_spec((CONV_WIDTH, D_XBC)),
            _const_spec((1, D_XBC)),
            _const_spec((1, DT_COLS)),
            _const_spec((1, DT_COLS)),
            _const_spec((1, D_SSM)),
            _const_spec((1, D_SSM)),
            _const_spec((D_MODEL // W_COL_BLOCK, D_MIX, W_COL_BLOCK)),
            _const_spec(state_shape),
            _const_spec(xtail_shape),
            _const_spec(utail_shape),
            _const_spec((1, D_MODEL)),
            _const_spec((D_FF // W_COL_BLOCK, D_MODEL, W_COL_BLOCK)),
            _const_spec((D_MODEL // W_COL_BLOCK, D_FF, W_COL_BLOCK)),
            _const_spec((1, D_MODEL)),
        ],
        out_specs=pl.BlockSpec((1, tm, D_MODEL), lambda n: tile_index(n - 1)),
        scratch_shapes=[pltpu.VMEM(state_shape, F32)] + 2 * [
            pltpu.VMEM((U_SLABS, tm + MAX_WINDOW, LANES), F32),
            pltpu.VMEM((X_SLABS, tm + CONV_TAIL, LANES), F32),
            pltpu.VMEM((tm, D_SSM), F32),
            pltpu.VMEM((tm, DT_COLS), F32),
        ] + [
            pltpu.VMEM((tm, D_SSM), F32),
            pltpu.VMEM((tm, D_SSM), BF16),
            pltpu.VMEM((tm, N_GROUPS * D_STATE), F32),
            pltpu.VMEM((tm, N_GROUPS * D_STATE), F32),
            pltpu.VMEM((tm, D_MIX), BF16),
            pltpu.VMEM((tm, D_MODEL), F32),
            pltpu.VMEM((tm, D_MODEL), F32),
        ],
        compiler_params=pltpu.CompilerParams(
            dimension_semantics=("arbitrary",), vmem_limit_bytes=VMEM_LIMIT),
        name="layer",
    )(x, x, nw, win, pool_w[0].astype(BF16), row(pool_scale[0]), convw, convb, dtb, alog, dskip,
      row(ssm_norm_w[0]), col_blocks(w_out[0]), s0, xtail0, utail0,
      row(norm_ffn_w[0]), col_blocks(w_ff1[0]), col_blocks(w_ff2[0]), row(norm_f_w))
```

```python
import functools

import jax
import jax.numpy as jnp
from jax import lax
from jax.experimental import pallas as pl
from jax.experimental.pallas import tpu as pltpu

D_MODEL = 1024
N_META = 16
POOL_WINDOWS = (2, 4, 8, 16)
D_POOL = 512
POOL_GROUP = 128
MAX_WINDOW = 16
D_SSM = 1536
HEAD_DIM = 64
N_HEADS = 24
N_GROUPS = 4
HEADS_PER_GROUP = 6
D_STATE = 128
CONV_WIDTH = 4
CHUNK = 128
D_XBC = 2560
D_MIX = 2048
D_FF = 4096
EPS = 1e-5

LANES = 128
SUBLANES = 8
DT_COLS = LANES
D_IN_PAD = D_POOL + D_SSM + D_XBC + DT_COLS
OFF_Z = D_POOL
OFF_XBC = D_POOL + D_SSM
OFF_DT = OFF_XBC + D_XBC
N_PAIRS = N_HEADS // 2
PAIRS_PER_GROUP = HEADS_PER_GROUP // 2
GROUP_WIDTH = HEADS_PER_GROUP * HEAD_DIM
CONV_TAIL = SUBLANES
CONV_COL_BLOCK = 512
GATE_COL_BLOCK = 512
X_SLABS = D_XBC // LANES
U_SLABS = D_POOL // LANES

TM_MIX = 256
W_COL_BLOCK = 512
FF_BLOCK = W_COL_BLOCK
HEAD_FILL = (2, 1) + (2, 1, 2, 1, 1)
assert len(HEAD_FILL) == 2 + D_XBC // CONV_COL_BLOCK
VMEM_LIMIT = 60 * 1024 * 1024

F32 = jnp.float32
BF16 = jnp.bfloat16
LOG2E = 1.4426950408889634


def _dot(a, b):
    return jnp.dot(a, b, preferred_element_type=F32)


def _dot_split_n(a, w_refs, rows):
    return jnp.concatenate([_dot(a, w_ref[rows, :]) for w_ref in w_refs], axis=1)


def _rms(x, w):
    return x * lax.rsqrt(jnp.mean(x * x, axis=-1, keepdims=True) + EPS) * w


def _silu(x):
    return x * (1.0 / (1.0 + jnp.exp(-x)))


def _softplus(x):
    return jnp.maximum(x, 0.0) + jnp.log(1.0 + jnp.exp(-jnp.abs(x)))


def _split3(x):
    hi = x.astype(BF16)
    r1 = x - hi.astype(F32)
    mid = r1.astype(BF16)
    lo = (r1 - mid.astype(F32)).astype(BF16)
    return hi, mid, lo


def _bcast_rows(row):
    return jnp.broadcast_to(row, (CHUNK, LANES))


def _causal_mask():
    row_id = lax.broadcasted_iota(jnp.int32, (CHUNK, CHUNK), 0)
    col_id = lax.broadcasted_iota(jnp.int32, (CHUNK, CHUNK), 1)
    return row_id >= col_id


def _ssd_decays(dt_chunks, a_row):
    ltri = jnp.where(_causal_mask(), 1.0, 0.0).astype(BF16)
    terms = [t for dt_c in dt_chunks for t in _split3(dt_c * (a_row * LOG2E))]
    sums = _dot(ltri, jnp.concatenate(terms, axis=1))
    out = []
    for c, dt_c in enumerate(dt_chunks):
        hi, mid, lo = (sums[:, (3 * c + k) * LANES:(3 * c + k + 1) * LANES] for k in range(3))
        a_cs = hi + mid + lo
        a_cs_t = a_cs.T
        src_t = a_cs_t - jnp.log2(dt_c.T)
        out.append((a_cs, src_t, jnp.exp2(a_cs_t[:, CHUNK - 1:CHUNK] - src_t)))
    return out


def _ssd_chunk(decays, xs_f32, xs_bf, b_c, c_c, z_c, dskip_ref, ssmw_ref, s_ref, y_out,
               between=lambda: None):
    a_cs, src_t, to_end_t = decays
    causal = _causal_mask()
    lower_half = lax.broadcasted_iota(jnp.int32, (CHUNK, LANES), 1) < HEAD_DIM

    for g in range(N_GROUPS):
        gs = slice(g * D_STATE, (g + 1) * D_STATE)
        b_g = b_c[:, gs]
        c_g = c_c[:, gs]
        b_t = b_g.T
        if y_out is not None:
            cb = lax.dot_general(c_g.astype(BF16), b_g.astype(BF16),
                                 (((1,), (1,)), ((), ())), preferred_element_type=F32)
        y_pairs = []
        for q in range(PAIRS_PER_GROUP):
            between()
            pair = g * PAIRS_PER_GROUP + q
            ps = slice(pair * LANES, (pair + 1) * LANES)
            x_pair = xs_bf[:, ps]
            s_old = s_ref[pair]
            rhs = jnp.concatenate([x_pair, s_old.astype(BF16)], axis=0)
            b_scaled, lhs, end_decay = [], [], []
            for half in range(2):
                h = 2 * pair + half
                col_b = jnp.broadcast_to(a_cs[:, h:h + 1], (CHUNK, LANES))
                end_decay.append(jnp.exp2(col_b[CHUNK - 1:CHUNK, :]))
                b_scaled.append((b_t * _bcast_rows(to_end_t[h:h + 1, :])).astype(BF16))
                if y_out is not None:
                    seg = jnp.where(causal, col_b - _bcast_rows(src_t[h:h + 1, :]), -jnp.inf)
                    m_h = cb * jnp.exp2(seg)
                    c_h = c_g * jnp.exp2(col_b)
                    lhs.append(jnp.concatenate([m_h.astype(BF16), c_h.astype(BF16)], axis=1))
            if y_out is not None:
                y_both = _dot(jnp.concatenate(lhs, axis=0), rhs)
                y_full = (y_both[0:CHUNK, :], y_both[CHUNK:2 * CHUNK, :])
            zeros = jnp.zeros_like(x_pair)
            x_split = jnp.concatenate([jnp.where(lower_half, x_pair, zeros),
                                       jnp.where(lower_half, zeros, x_pair)], axis=0)
            st = _dot(jnp.concatenate(b_scaled, axis=1), x_split)
            s_ref[pair] = s_old * jnp.where(lower_half[0:1, :], end_decay[0], end_decay[1]) + st
            if y_out is not None:
                y_pairs.append(jnp.where(lower_half, y_full[0], y_full[1])
                               + dskip_ref[:, ps] * xs_f32[:, ps])
        if y_out is not None:
            ws = slice(g * GROUP_WIDTH, (g + 1) * GROUP_WIDTH)
            y_g = jnp.concatenate(y_pairs, axis=1) * _silu(z_c[:, ws])
            y_g = y_g * lax.rsqrt(jnp.mean(y_g * y_g, axis=-1, keepdims=True) + EPS) * ssmw_ref[:, ws]
            y_out(g, y_g.astype(BF16))


def _project_xbc(hn, tm, win_ref, xbuf, base, cb):
    res = _dot(hn, win_ref[:, OFF_XBC + cb:OFF_XBC + cb + CONV_COL_BLOCK])
    for i in range(0, CONV_COL_BLOCK, LANES):
        xbuf[base + (cb + i) // LANES, CONV_TAIL:CONV_TAIL + tm, :] = res[:, i:i + LANES]


def _conv_block(tm, convw_ref, convb_ref, xbuf, base, cb, xs_f32, xs_bf, b_buf, c_buf):
    for c0 in range(cb, cb + CONV_COL_BLOCK, LANES):
        cols = slice(c0, c0 + LANES)
        slab = base + c0 // LANES
        acc = convb_ref[:, cols] + convw_ref[CONV_WIDTH - 1:CONV_WIDTH, cols] * xbuf[slab, CONV_TAIL:CONV_TAIL + tm, :]
        for k in range(CONV_WIDTH - 1):
            start = CONV_TAIL - (CONV_WIDTH - 1) + k
            acc = acc + convw_ref[k:k + 1, cols] * xbuf[slab, start:start + tm, :]
        v = _silu(acc)
        if c0 < D_SSM:
            xs_f32[:, cols] = v
            xs_bf[:, cols] = v.astype(BF16)
        elif c0 < D_SSM + N_GROUPS * D_STATE:
            b_buf[:, c0 - D_SSM:c0 - D_SSM + LANES] = v
        else:
            off = c0 - D_SSM - N_GROUPS * D_STATE
            c_buf[:, off:off + LANES] = v


def _dt_and_a(dt_raw, dtb_ref, alog_ref):
    return _softplus(dt_raw + dtb_ref[...]), -jnp.exp(alog_ref[...])


def _meta_kernel(x_ref, nw_ref, win_ref, convw_ref, convb_ref, dtb_ref, alog_ref,
                 s_out, xtail_out, utail_out,
                 xbuf, xs_f32, xs_bf, b_buf, c_buf):
    s_out[...] = jnp.zeros_like(s_out)
    xbuf[:, 0:CONV_TAIL, :] = jnp.zeros((X_SLABS, CONV_TAIL, LANES), F32)
    hn = _rms(x_ref[...], nw_ref[...]).astype(BF16)
    u = _dot(hn, win_ref[:, 0:D_POOL])
    for gi in range(U_SLABS):
        utail_out[gi] = u[CHUNK - MAX_WINDOW:CHUNK, gi * POOL_GROUP:(gi + 1) * POOL_GROUP]
    for cb in range(0, D_XBC, CONV_COL_BLOCK):
        _project_xbc(hn, CHUNK, win_ref, xbuf, 0, cb)
        _conv_block(CHUNK, convw_ref, convb_ref, xbuf, 0, cb, xs_f32, xs_bf, b_buf, c_buf)
    xtail_out[...] = xbuf[:, CHUNK:CHUNK + CONV_TAIL, :]
    dt, a_row = _dt_and_a(_dot(hn, win_ref[:, OFF_DT:D_IN_PAD]), dtb_ref, alog_ref)
    valid = lax.broadcasted_iota(jnp.int32, (CHUNK, DT_COLS), 0) >= CHUNK - N_META
    dt = jnp.where(valid, dt, 0.0)
    _ssd_chunk(_ssd_decays([dt], a_row)[0], None, xs_bf[...], b_buf[...], c_buf[...], None, None, None, s_out, None)


def _spread(n_items, n_slots):
    return [-(-(i + 1) * n_items // n_slots) + (-i * n_items // n_slots) for i in range(n_slots)]


def _in_projection(hn, tm, win_ref, slot):
    ubuf, xbuf, zbuf, dtbuf = slot

    def dt_unit():
        dtbuf[...] = _dot(hn, win_ref[:, OFF_DT:D_IN_PAD])

    def pool_unit():
        u = _dot(hn, win_ref[:, 0:D_POOL])
        for gi in range(U_SLABS):
            ubuf[gi, MAX_WINDOW:MAX_WINDOW + tm, :] = u[:, gi * LANES:(gi + 1) * LANES]

    def gate_unit(j):
        zbuf[:, j:j + GATE_COL_BLOCK] = _dot(hn, win_ref[:, OFF_Z + j:OFF_Z + j + GATE_COL_BLOCK])

    units = [dt_unit, pool_unit]
    units += [functools.partial(_project_xbc, hn, tm, win_ref, xbuf, 0, cb)
              for cb in range(0, D_XBC, CONV_COL_BLOCK)]
    units += [functools.partial(gate_unit, j) for j in range(0, D_SSM, GATE_COL_BLOCK)]
    return units


def _layer_kernel(*refs, tiles_per_seq):
    refs = list(refs)
    take = lambda k: [refs.pop(0) for _ in range(k)]
    (xn_ref, xc_ref, nw_ref, win_ref, poolw_ref, pscale_ref, convw_ref, convb_ref, dtb_ref,
     alog_ref, dskip_ref, ssmw_ref) = take(12)
    wout_ref = take(D_MODEL // W_COL_BLOCK)
    s0_ref, xtail0_ref, utail0_ref, nw_ffn_ref = take(4)
    w1_ref = take(D_FF // W_COL_BLOCK)
    w2_ref = take(D_MODEL // W_COL_BLOCK)
    nfw_ref, o_ref = take(2)
    (s_ref, ubuf0, xbuf0, zbuf0, dtbuf0, ubuf1, xbuf1, zbuf1, dtbuf1,
     xs_f32, xs_bf, b_buf, c_buf, y_buf, h1_buf, h2_buf) = refs
    n = pl.program_id(0)
    slots = ((ubuf0, xbuf0, zbuf0, dtbuf0), (ubuf1, xbuf1, zbuf1, dtbuf1))

    @pl.when(n == 0)
    def _():
        h1_buf[...] = jnp.zeros_like(h1_buf)
        ubuf1[...] = jnp.zeros_like(ubuf1)
        xbuf1[...] = jnp.zeros_like(xbuf1)
        hn0 = _rms(xc_ref[0], nw_ref[...]).astype(BF16)
        for unit in _in_projection(hn0, TM_MIX, win_ref, slots[0]):
            unit()

    first = n % tiles_per_seq == 0

    @pl.when(first)
    def _():
        s_ref[...] = s0_ref[...]

    args = (xn_ref, xc_ref, nw_ref, win_ref, poolw_ref, pscale_ref, convw_ref, convb_ref, dtb_ref,
            alog_ref, dskip_ref, ssmw_ref, wout_ref, xtail0_ref, utail0_ref,
            nw_ffn_ref, w1_ref, w2_ref, nfw_ref, o_ref,
            s_ref, xs_f32, xs_bf, b_buf, c_buf, y_buf, h1_buf, h2_buf)
    for parity in range(2):
        pl.when(n % 2 == parity)(functools.partial(_layer_step, first, slots[parity], slots[1 - parity], *args))


def _layer_step(first, cur, nxt, xn_ref, xc_ref, nw_ref, win_ref, poolw_ref, pscale_ref, convw_ref,
                convb_ref, dtb_ref, alog_ref, dskip_ref, ssmw_ref, wout_ref, xtail0_ref, utail0_ref,
                nw_ffn_ref, w1_ref, w2_ref, nfw_ref, o_ref,
                s_ref, xs_f32, xs_bf, b_buf, c_buf, y_buf, h1_buf, h2_buf):
    tm = TM_MIX
    ubuf, xbuf, zbuf, dtbuf = cur

    xbuf[:, 0:CONV_TAIL, :] = jnp.where(first, xtail0_ref[...], nxt[1][:, tm:tm + CONV_TAIL, :])
    ubuf[:, 0:MAX_WINDOW, :] = jnp.where(first, utail0_ref[...], nxt[0][:, tm:tm + MAX_WINDOW, :])

    hn_next = _rms(xn_ref[0], nw_ref[...]).astype(BF16)
    h_prev = h1_buf[...]
    hn_prev = _rms(h_prev, nw_ffn_ref[...]).astype(BF16)
    h2_buf[...] = h_prev
    x = xc_ref[0]
    acts = {}

    def mlp_up(j):
        acts[j] = jnp.square(jnp.maximum(_dot(hn_prev, w1_ref[j][...]), 0.0)).astype(BF16)

    def mlp_down(j):
        h2_buf[...] += _dot_split_n(acts.pop(j), w2_ref, slice(j * FF_BLOCK, (j + 1) * FF_BLOCK))

    n_ff = D_FF // FF_BLOCK
    mlp_units = [functools.partial(mlp_up, 0)]
    for j in range(1, n_ff):
        mlp_units += [functools.partial(mlp_up, j), functools.partial(mlp_down, j - 1)]
    mlp_units.append(functools.partial(mlp_down, n_ff - 1))
    proj_units = _in_projection(hn_next, tm, win_ref, nxt)
    fillers = []
    for i, k in enumerate(_spread(len(proj_units), len(mlp_units))):
        fillers.append(mlp_units[i])
        fillers += [proj_units.pop(0) for _ in range(k)]
    n_pairs = (tm // CHUNK) * N_PAIRS
    quota = iter(HEAD_FILL + tuple(_spread(len(fillers) - sum(HEAD_FILL), n_pairs)))

    def fill():
        for _ in range(next(quota)):
            fillers.pop(0)()

    fill()
    dt, a_row = _dt_and_a(dtbuf[...], dtb_ref, alog_ref)
    decays = _ssd_decays([dt[c * CHUNK:(c + 1) * CHUNK, :] for c in range(tm // CHUNK)], a_row)

    fill()
    for gi, w in enumerate(POOL_WINDOWS):
        cols = slice(gi * POOL_GROUP, (gi + 1) * POOL_GROUP)
        tok = ubuf[gi, MAX_WINDOW:MAX_WINDOW + tm, :]
        acc = tok
        for j in range(1, w):
            acc = acc + ubuf[gi, MAX_WINDOW - j:MAX_WINDOW - j + tm, :]
        pooled = acc * (1.0 / w) - tok
        mixed = _dot(pooled.astype(BF16), poolw_ref[gi])
        y_buf[:, cols] = (mixed * pscale_ref[:, cols]).astype(BF16)

    for cb in range(0, D_XBC, CONV_COL_BLOCK):
        fill()
        _conv_block(tm, convw_ref, convb_ref, xbuf, 0, cb, xs_f32, xs_bf, b_buf, c_buf)

    for c in range(tm // CHUNK):
        rows = slice(c * CHUNK, (c + 1) * CHUNK)

        def store_y(g, val, rows=rows):
            y_buf[rows, D_POOL + g * GROUP_WIDTH:D_POOL + (g + 1) * GROUP_WIDTH] = val

        _ssd_chunk(decays[c], xs_f32[rows, :], xs_bf[rows, :], b_buf[rows, :], c_buf[rows, :],
                   zbuf[rows, :], dskip_ref, ssmw_ref, s_ref, store_y, between=fill)
        h1_buf[rows, :] = x[rows, :] + _dot_split_n(y_buf[rows, :], wout_ref, slice(None))

    assert not fillers and not acts and next(quota, None) is None
    o_ref[0] = _rms(h2_buf[...], nfw_ref[...])


def _const_spec(shape):
    nd = len(shape)
    return pl.BlockSpec(shape, lambda *_: (0,) * nd, pipeline_mode=pl.Buffered(1))


def kernel(x, meta, norm_mix_w, w_in, pool_w, pool_scale, conv_w, conv_b, dt_bias, a_log, d_skip,
           ssm_norm_w, w_out, norm_ffn_w, w_ff1, w_ff2, norm_f_w):
    bsz, seq, d_model = x.shape
    assert d_model == D_MODEL and seq % TM_MIX == 0
    assert w_in.shape == (1, D_MODEL, OFF_DT + N_HEADS) and meta.shape == (N_META, D_MODEL)

    win = jnp.concatenate([w_in[0], jnp.zeros((D_MODEL, DT_COLS - N_HEADS), F32)], axis=1).astype(BF16)
    pad_heads = lambda v: jnp.concatenate([v, jnp.zeros((DT_COLS - N_HEADS,), F32)])[None, :]
    dtb = pad_heads(dt_bias[0])
    alog = pad_heads(a_log[0])
    dskip = jnp.repeat(d_skip[0], HEAD_DIM)[None, :]
    row = lambda v: v[None, :]
    nw = row(norm_mix_w[0])
    convw = conv_w[0]
    convb = row(conv_b[0])
    meta_pad = jnp.concatenate([jnp.zeros((CHUNK - N_META, D_MODEL), F32), meta.astype(F32)], axis=0)

    def col_block_specs(k, n):
        return [pl.BlockSpec((k, W_COL_BLOCK), lambda _, j=j: (0, j), pipeline_mode=pl.Buffered(1))
                for j in range(n // W_COL_BLOCK)]

    wout_b, w1_b, w2_b = w_out[0].astype(BF16), w_ff1[0].astype(BF16), w_ff2[0].astype(BF16)

    state_shape = (N_PAIRS, D_STATE, LANES)
    xtail_shape = (X_SLABS, CONV_TAIL, LANES)
    utail_shape = (U_SLABS, MAX_WINDOW, LANES)
    s0, xtail0, utail0 = pl.pallas_call(
        _meta_kernel,
        out_shape=(jax.ShapeDtypeStruct(state_shape, F32),
                   jax.ShapeDtypeStruct(xtail_shape, F32),
                   jax.ShapeDtypeStruct(utail_shape, F32)),
        scratch_shapes=[
            pltpu.VMEM((X_SLABS, CHUNK + CONV_TAIL, LANES), F32),
            pltpu.VMEM((CHUNK, D_SSM), F32),
            pltpu.VMEM((CHUNK, D_SSM), BF16),
            pltpu.VMEM((CHUNK, N_GROUPS * D_STATE), F32),
            pltpu.VMEM((CHUNK, N_GROUPS * D_STATE), F32),
        ],
        compiler_params=pltpu.CompilerParams(vmem_limit_bytes=VMEM_LIMIT),
        name="meta_prologue",
    )(meta_pad, nw, win, convw, convb, dtb, alog)

    tm = TM_MIX
    tiles_per_seq = seq // tm
    n_tiles = bsz * tiles_per_seq

    def tile_index(m):
        m = jnp.clip(m, 0, n_tiles - 1)
        return (m // tiles_per_seq, m % tiles_per_seq, 0)

    return pl.pallas_call(
        functools.partial(_layer_kernel, tiles_per_seq=tiles_per_seq),
        out_shape=jax.ShapeDtypeStruct((bsz, seq, D_MODEL), F32),
        grid=(n_tiles + 1,),
        in_specs=[
            pl.BlockSpec((1, tm, D_MODEL), lambda n: tile_index(n + 1)),
            pl.BlockSpec((1, tm, D_MODEL), lambda n: tile_index(n)),
            _const_spec((1, D_MODEL)),
            _const_spec((D_MODEL, D_IN_PAD)),
            _const_spec((N_GROUPS, POOL_GROUP, POOL_GROUP)),
            _const_spec((1, D_POOL)),
            _const_spec((CONV_WIDTH, D_XBC)),
            _const_spec((1, D_XBC)),
            _const_spec((1, DT_COLS)),
            _const_spec((1, DT_COLS)),
            _const_spec((1, D_SSM)),
            _const_spec((1, D_SSM)),
            *col_block_specs(D_MIX, D_MODEL),
            _const_spec(state_shape),
            _const_spec(xtail_shape),
            _const_spec(utail_shape),
            _const_spec((1, D_MODEL)),
            *col_block_specs(D_MODEL, D_FF),
            *col_block_specs(D_FF, D_MODEL),
            _const_spec((1, D_MODEL)),
        ],
        out_specs=pl.BlockSpec((1, tm, D_MODEL), lambda n: tile_index(n - 1)),
        scratch_shapes=[pltpu.VMEM(state_shape, F32)] + 2 * [
            pltpu.VMEM((U_SLABS, tm + MAX_WINDOW, LANES), F32),
            pltpu.VMEM((X_SLABS, tm + CONV_TAIL, LANES), F32),
            pltpu.VMEM((tm, D_SSM), F32),
            pltpu.VMEM((tm, DT_COLS), F32),
        ] + [
            pltpu.VMEM((tm, D_SSM), F32),
            pltpu.VMEM((tm, D_SSM), BF16),
            pltpu.VMEM((tm, N_GROUPS * D_STATE), F32),
            pltpu.VMEM((tm, N_GROUPS * D_STATE), F32),
            pltpu.VMEM((tm, D_MIX), BF16),
            pltpu.VMEM((tm, D_MODEL), F32),
            pltpu.VMEM((tm, D_MODEL), F32),
        ],
        compiler_params=pltpu.CompilerParams(
            dimension_semantics=("arbitrary",), vmem_limit_bytes=VMEM_LIMIT),
        name="layer",
    )(x, x, nw, win, pool_w[0].astype(BF16), row(pool_scale[0]), convw, convb, dtb, alog, dskip,
      row(ssm_norm_w[0]), *[wout_b] * (D_MODEL // W_COL_BLOCK), s0, xtail0, utail0,
      row(norm_ffn_w[0]), *[w1_b] * (D_FF // W_COL_BLOCK), *[w2_b] * (D_MODEL // W_COL_BLOCK),
      row(norm_f_w))
```

```python
import functools

import jax
import jax.numpy as jnp
from jax import lax
from jax.experimental import pallas as pl
from jax.experimental.pallas import tpu as pltpu

D_MODEL = 1024
N_META = 16
POOL_WINDOWS = (2, 4, 8, 16)
D_POOL = 512
POOL_GROUP = 128
MAX_WINDOW = 16
D_SSM = 1536
HEAD_DIM = 64
N_HEADS = 24
N_GROUPS = 4
HEADS_PER_GROUP = 6
D_STATE = 128
CONV_WIDTH = 4
CHUNK = 128
D_XBC = 2560
D_MIX = 2048
D_FF = 4096
EPS = 1e-5

LANES = 128
SUBLANES = 8
DT_COLS = LANES
OFF_Z = D_POOL
OFF_XBC = D_POOL + D_SSM
OFF_DT = OFF_XBC + D_XBC
N_PAIRS = N_HEADS // 2
PAIRS_PER_GROUP = HEADS_PER_GROUP // 2
GROUP_WIDTH = HEADS_PER_GROUP * HEAD_DIM
CONV_TAIL = SUBLANES
CONV_COL_BLOCK = 512
GATE_COL_BLOCK = 512
X_SLABS = D_XBC // LANES
U_SLABS = D_POOL // LANES

TM_MIX = 256
W_COL_BLOCK = 512
FF_BLOCK = W_COL_BLOCK
HEAD_FILL = (2, 1) + (2, 1, 2, 1, 1)
assert len(HEAD_FILL) == 2 + D_XBC // CONV_COL_BLOCK
VMEM_LIMIT = 60 * 1024 * 1024

F32 = jnp.float32
BF16 = jnp.bfloat16
LOG2E = 1.4426950408889634


def _dot(a, b):
    return jnp.dot(a, b, preferred_element_type=F32)


def _dot_split_n(a, w_refs, rows):
    return jnp.concatenate([_dot(a, w_ref[rows, :]) for w_ref in w_refs], axis=1)


def _rms(x, w):
    return x * lax.rsqrt(jnp.mean(x * x, axis=-1, keepdims=True) + EPS) * w


def _silu(x):
    return x * (1.0 / (1.0 + jnp.exp(-x)))


def _softplus(x):
    return jnp.maximum(x, 0.0) + jnp.log(1.0 + jnp.exp(-jnp.abs(x)))


def _split3(x):
    hi = x.astype(BF16)
    r1 = x - hi.astype(F32)
    mid = r1.astype(BF16)
    lo = (r1 - mid.astype(F32)).astype(BF16)
    return hi, mid, lo


def _bcast_rows(row):
    return jnp.broadcast_to(row, (CHUNK, LANES))


def _causal_mask():
    row_id = lax.broadcasted_iota(jnp.int32, (CHUNK, CHUNK), 0)
    col_id = lax.broadcasted_iota(jnp.int32, (CHUNK, CHUNK), 1)
    return row_id >= col_id


def _ssd_decays(dt_chunks, a_row):
    ltri = jnp.where(_causal_mask(), 1.0, 0.0).astype(BF16)
    terms = [t for dt_c in dt_chunks for t in _split3(dt_c * (a_row * LOG2E))]
    sums = _dot(ltri, jnp.concatenate(terms, axis=1))
    out = []
    for c, dt_c in enumerate(dt_chunks):
        hi, mid, lo = (sums[:, (3 * c + k) * LANES:(3 * c + k + 1) * LANES] for k in range(3))
        a_cs = hi + mid + lo
        a_cs_t = a_cs.T
        src_t = a_cs_t - jnp.log2(dt_c.T)
        out.append((a_cs, src_t, jnp.exp2(a_cs_t[:, CHUNK - 1:CHUNK] - src_t)))
    return out


def _ssd_chunk(decays, xs_f32, xs_bf, b_c, c_c, z_c, dskip_ref, ssmw_ref, s_ref, y_out,
               between=lambda: None):
    a_cs, src_t, to_end_t = decays
    causal = _causal_mask()
    lower_half = lax.broadcasted_iota(jnp.int32, (CHUNK, LANES), 1) < HEAD_DIM

    for g in range(N_GROUPS):
        gs = slice(g * D_STATE, (g + 1) * D_STATE)
        b_g = b_c[:, gs]
        c_g = c_c[:, gs]
        b_t = b_g.T
        if y_out is not None:
            cb = lax.dot_general(c_g.astype(BF16), b_g.astype(BF16),
                                 (((1,), (1,)), ((), ())), preferred_element_type=F32)
        y_pairs = []
        for q in range(PAIRS_PER_GROUP):
            between()
            pair = g * PAIRS_PER_GROUP + q
            ps = slice(pair * LANES, (pair + 1) * LANES)
            x_pair = xs_bf[:, ps]
            s_old = s_ref[pair]
            rhs = jnp.concatenate([x_pair, s_old.astype(BF16)], axis=0)
            b_scaled, lhs, end_decay = [], [], []
            for half in range(2):
                h = 2 * pair + half
                col_b = jnp.broadcast_to(a_cs[:, h:h + 1], (CHUNK, LANES))
                end_decay.append(jnp.exp2(col_b[CHUNK - 1:CHUNK, :]))
                b_scaled.append((b_t * _bcast_rows(to_end_t[h:h + 1, :])).astype(BF16))
                if y_out is not None:
                    seg = jnp.where(causal, col_b - _bcast_rows(src_t[h:h + 1, :]), -jnp.inf)
                    m_h = cb * jnp.exp2(seg)
                    c_h = c_g * jnp.exp2(col_b)
                    lhs.append(jnp.concatenate([m_h.astype(BF16), c_h.astype(BF16)], axis=1))
            if y_out is not None:
                y_both = _dot(jnp.concatenate(lhs, axis=0), rhs)
                y_full = (y_both[0:CHUNK, :], y_both[CHUNK:2 * CHUNK, :])
            zeros = jnp.zeros_like(x_pair)
            x_split = jnp.concatenate([jnp.where(lower_half, x_pair, zeros),
                                       jnp.where(lower_half, zeros, x_pair)], axis=0)
            st = _dot(jnp.concatenate(b_scaled, axis=1), x_split)
            s_ref[pair] = s_old * jnp.where(lower_half[0:1, :], end_decay[0], end_decay[1]) + st
            if y_out is not None:
                y_pairs.append(jnp.where(lower_half, y_full[0], y_full[1])
                               + dskip_ref[:, ps] * xs_f32[:, ps])
        if y_out is not None:
            ws = slice(g * GROUP_WIDTH, (g + 1) * GROUP_WIDTH)
            y_g = jnp.concatenate(y_pairs, axis=1) * _silu(z_c[:, ws])
            y_g = y_g * lax.rsqrt(jnp.mean(y_g * y_g, axis=-1, keepdims=True) + EPS) * ssmw_ref[:, ws]
            y_out(g, y_g.astype(BF16))


def _project_xbc(hn, tm, win_ref, xbuf, base, cb):
    res = _dot(hn, win_ref[:, OFF_XBC + cb:OFF_XBC + cb + CONV_COL_BLOCK])
    for i in range(0, CONV_COL_BLOCK, LANES):
        xbuf[base + (cb + i) // LANES, CONV_TAIL:CONV_TAIL + tm, :] = res[:, i:i + LANES]


def _conv_block(tm, convw_ref, convb_ref, xbuf, base, cb, xs_f32, xs_bf, b_buf, c_buf):
    for c0 in range(cb, cb + CONV_COL_BLOCK, LANES):
        cols = slice(c0, c0 + LANES)
        slab = base + c0 // LANES
        acc = convb_ref[:, cols] + convw_ref[CONV_WIDTH - 1:CONV_WIDTH, cols] * xbuf[slab, CONV_TAIL:CONV_TAIL + tm, :]
        for k in range(CONV_WIDTH - 1):
            start = CONV_TAIL - (CONV_WIDTH - 1) + k
            acc = acc + convw_ref[k:k + 1, cols] * xbuf[slab, start:start + tm, :]
        v = _silu(acc)
        if c0 < D_SSM:
            xs_f32[:, cols] = v
            xs_bf[:, cols] = v.astype(BF16)
        elif c0 < D_SSM + N_GROUPS * D_STATE:
            b_buf[:, c0 - D_SSM:c0 - D_SSM + LANES] = v
        else:
            off = c0 - D_SSM - N_GROUPS * D_STATE
            c_buf[:, off:off + LANES] = v


def _dt_and_a(dt_raw, dtb_ref, alog_ref):
    return _softplus(dt_raw + dtb_ref[...]), -jnp.exp(alog_ref[...])


def _meta_kernel(x_ref, nw_ref, win_ref, wdt_ref, convw_ref, convb_ref, dtb_ref, alog_ref,
                 s_out, xtail_out, utail_out,
                 xbuf, xs_f32, xs_bf, b_buf, c_buf):
    s_out[...] = jnp.zeros_like(s_out)
    xbuf[:, 0:CONV_TAIL, :] = jnp.zeros((X_SLABS, CONV_TAIL, LANES), F32)
    hn = _rms(x_ref[...], nw_ref[...]).astype(BF16)
    u = _dot(hn, win_ref[:, 0:D_POOL])
    for gi in range(U_SLABS):
        utail_out[gi] = u[CHUNK - MAX_WINDOW:CHUNK, gi * POOL_GROUP:(gi + 1) * POOL_GROUP]
    for cb in range(0, D_XBC, CONV_COL_BLOCK):
        _project_xbc(hn, CHUNK, win_ref, xbuf, 0, cb)
        _conv_block(CHUNK, convw_ref, convb_ref, xbuf, 0, cb, xs_f32, xs_bf, b_buf, c_buf)
    xtail_out[...] = xbuf[:, CHUNK:CHUNK + CONV_TAIL, :]
    dt, a_row = _dt_and_a(_dot(hn, wdt_ref[...]), dtb_ref, alog_ref)
    valid = lax.broadcasted_iota(jnp.int32, (CHUNK, DT_COLS), 0) >= CHUNK - N_META
    dt = jnp.where(valid, dt, 0.0)
    _ssd_chunk(_ssd_decays([dt], a_row)[0], None, xs_bf[...], b_buf[...], c_buf[...], None, None, None, s_out, None)


def _spread(n_items, n_slots):
    return [-(-(i + 1) * n_items // n_slots) + (-i * n_items // n_slots) for i in range(n_slots)]


def _in_projection(hn, tm, win_refs, slot):
    win_ref, wdt_ref = win_refs
    ubuf, xbuf, zbuf, dtbuf = slot

    def dt_unit():
        dtbuf[...] = _dot(hn, wdt_ref[...])

    def pool_unit():
        u = _dot(hn, win_ref[:, 0:D_POOL])
        for gi in range(U_SLABS):
            ubuf[gi, MAX_WINDOW:MAX_WINDOW + tm, :] = u[:, gi * LANES:(gi + 1) * LANES]

    def gate_unit(j):
        zbuf[:, j:j + GATE_COL_BLOCK] = _dot(hn, win_ref[:, OFF_Z + j:OFF_Z + j + GATE_COL_BLOCK])

    units = [dt_unit, pool_unit]
    units += [functools.partial(_project_xbc, hn, tm, win_ref, xbuf, 0, cb)
              for cb in range(0, D_XBC, CONV_COL_BLOCK)]
    units += [functools.partial(gate_unit, j) for j in range(0, D_SSM, GATE_COL_BLOCK)]
    return units


def _layer_kernel(*refs, tiles_per_seq):
    refs = list(refs)
    take = lambda k: [refs.pop(0) for _ in range(k)]
    (xn_ref, xc_ref, nw_ref, win_ref, wdt_ref, poolw_ref, pscale_ref, convw_ref, convb_ref, dtb_ref,
     alog_ref, dskip_ref, ssmw_ref) = take(13)
    win_ref = (win_ref, wdt_ref)
    wout_ref = take(D_MODEL // W_COL_BLOCK)
    s0_ref, xtail0_ref, utail0_ref, nw_ffn_ref = take(4)
    w1_ref = take(D_FF // W_COL_BLOCK)
    w2_ref = take(D_MODEL // W_COL_BLOCK)
    nfw_ref, o_ref = take(2)
    (s_ref, ubuf0, xbuf0, zbuf0, dtbuf0, ubuf1, xbuf1, zbuf1, dtbuf1,
     xs_f32, xs_bf, b_buf, c_buf, y_buf, h1_buf, h2_buf) = refs
    n = pl.program_id(0)
    slots = ((ubuf0, xbuf0, zbuf0, dtbuf0), (ubuf1, xbuf1, zbuf1, dtbuf1))

    @pl.when(n == 0)
    def _():
        h1_buf[...] = jnp.zeros_like(h1_buf)
        ubuf1[...] = jnp.zeros_like(ubuf1)
        xbuf1[...] = jnp.zeros_like(xbuf1)
        hn0 = _rms(xc_ref[0], nw_ref[...]).astype(BF16)
        for unit in _in_projection(hn0, TM_MIX, win_ref, slots[0]):
            unit()

    first = n % tiles_per_seq == 0

    @pl.when(first)
    def _():
        s_ref[...] = s0_ref[...]

    args = (xn_ref, xc_ref, nw_ref, win_ref, poolw_ref, pscale_ref, convw_ref, convb_ref, dtb_ref,
            alog_ref, dskip_ref, ssmw_ref, wout_ref, xtail0_ref, utail0_ref,
            nw_ffn_ref, w1_ref, w2_ref, nfw_ref, o_ref,
            s_ref, xs_f32, xs_bf, b_buf, c_buf, y_buf, h1_buf, h2_buf)
    for parity in range(2):
        pl.when(n % 2 == parity)(functools.partial(_layer_step, first, slots[parity], slots[1 - parity], *args))


def _layer_step(first, cur, nxt, xn_ref, xc_ref, nw_ref, win_ref, poolw_ref, pscale_ref, convw_ref,
                convb_ref, dtb_ref, alog_ref, dskip_ref, ssmw_ref, wout_ref, xtail0_ref, utail0_ref,
                nw_ffn_ref, w1_ref, w2_ref, nfw_ref, o_ref,
                s_ref, xs_f32, xs_bf, b_buf, c_buf, y_buf, h1_buf, h2_buf):
    tm = TM_MIX
    ubuf, xbuf, zbuf, dtbuf = cur

    xbuf[:, 0:CONV_TAIL, :] = jnp.where(first, xtail0_ref[...], nxt[1][:, tm:tm + CONV_TAIL, :])
    ubuf[:, 0:MAX_WINDOW, :] = jnp.where(first, utail0_ref[...], nxt[0][:, tm:tm + MAX_WINDOW, :])

    hn_next = _rms(xn_ref[0], nw_ref[...]).astype(BF16)
    h_prev = h1_buf[...]
    hn_prev = _rms(h_prev, nw_ffn_ref[...]).astype(BF16)
    h2_buf[...] = h_prev
    x = xc_ref[0]
    acts = {}

    def mlp_up(j):
        acts[j] = jnp.square(jnp.maximum(_dot(hn_prev, w1_ref[j][...]), 0.0)).astype(BF16)

    def mlp_down(j):
        h2_buf[...] += _dot_split_n(acts.pop(j), w2_ref, slice(j * FF_BLOCK, (j + 1) * FF_BLOCK))

    n_ff = D_FF // FF_BLOCK
    mlp_units = [functools.partial(mlp_up, 0)]
    for j in range(1, n_ff):
        mlp_units += [functools.partial(mlp_up, j), functools.partial(mlp_down, j - 1)]
    mlp_units.append(functools.partial(mlp_down, n_ff - 1))
    proj_units = _in_projection(hn_next, tm, win_ref, nxt)
    fillers = []
    for i, k in enumerate(_spread(len(proj_units), len(mlp_units))):
        fillers.append(mlp_units[i])
        fillers += [proj_units.pop(0) for _ in range(k)]
    n_pairs = (tm // CHUNK) * N_PAIRS
    quota = iter(HEAD_FILL + tuple(_spread(len(fillers) - sum(HEAD_FILL), n_pairs)))

    def fill():
        for _ in range(next(quota)):
            fillers.pop(0)()

    fill()
    dt, a_row = _dt_and_a(dtbuf[...], dtb_ref, alog_ref)
    decays = _ssd_decays([dt[c * CHUNK:(c + 1) * CHUNK, :] for c in range(tm // CHUNK)], a_row)

    fill()
    for gi, w in enumerate(POOL_WINDOWS):
        cols = slice(gi * POOL_GROUP, (gi + 1) * POOL_GROUP)
        tok = ubuf[gi, MAX_WINDOW:MAX_WINDOW + tm, :]
        acc = tok
        for j in range(1, w):
            acc = acc + ubuf[gi, MAX_WINDOW - j:MAX_WINDOW - j + tm, :]
        pooled = acc * (1.0 / w) - tok
        mixed = _dot(pooled.astype(BF16), poolw_ref[gi])
        y_buf[:, cols] = (mixed * pscale_ref[:, cols]).astype(BF16)

    for cb in range(0, D_XBC, CONV_COL_BLOCK):
        fill()
        _conv_block(tm, convw_ref, convb_ref, xbuf, 0, cb, xs_f32, xs_bf, b_buf, c_buf)

    for c in range(tm // CHUNK):
        rows = slice(c * CHUNK, (c + 1) * CHUNK)

        def store_y(g, val, rows=rows):
            y_buf[rows, D_POOL + g * GROUP_WIDTH:D_POOL + (g + 1) * GROUP_WIDTH] = val

        _ssd_chunk(decays[c], xs_f32[rows, :], xs_bf[rows, :], b_buf[rows, :], c_buf[rows, :],
                   zbuf[rows, :], dskip_ref, ssmw_ref, s_ref, store_y, between=fill)
        h1_buf[rows, :] = x[rows, :] + _dot_split_n(y_buf[rows, :], wout_ref, slice(None))

    assert not fillers and not acts and next(quota, None) is None
    o_ref[0] = _rms(h2_buf[...], nfw_ref[...])


def _const_spec(shape):
    nd = len(shape)
    return pl.BlockSpec(shape, lambda *_: (0,) * nd, pipeline_mode=pl.Buffered(1))


def kernel(x, meta, norm_mix_w, w_in, pool_w, pool_scale, conv_w, conv_b, dt_bias, a_log, d_skip,
           ssm_norm_w, w_out, norm_ffn_w, w_ff1, w_ff2, norm_f_w):
    bsz, seq, d_model = x.shape
    assert d_model == D_MODEL and seq % TM_MIX == 0
    assert w_in.shape == (1, D_MODEL, OFF_DT + N_HEADS) and meta.shape == (N_META, D_MODEL)

    win = w_in[0][:, :OFF_DT].astype(BF16)
    wdt = jnp.pad(w_in[0][:, OFF_DT:], ((0, 0), (0, DT_COLS - N_HEADS))).astype(BF16)
    pad_heads = lambda v: jnp.concatenate([v, jnp.zeros((DT_COLS - N_HEADS,), F32)])[None, :]
    dtb = pad_heads(dt_bias[0])
    alog = pad_heads(a_log[0])
    dskip = jnp.repeat(d_skip[0], HEAD_DIM)[None, :]
    row = lambda v: v[None, :]
    nw = row(norm_mix_w[0])
    convw = conv_w[0]
    convb = row(conv_b[0])
    meta_pad = jnp.concatenate([jnp.zeros((CHUNK - N_META, D_MODEL), F32), meta.astype(F32)], axis=0)

    def col_block_specs(k, n):
        return [pl.BlockSpec((k, W_COL_BLOCK), lambda _, j=j: (0, j), pipeline_mode=pl.Buffered(1))
                for j in range(n // W_COL_BLOCK)]

    wout_b, w1_b, w2_b = w_out[0].astype(BF16), w_ff1[0].astype(BF16), w_ff2[0].astype(BF16)

    state_shape = (N_PAIRS, D_STATE, LANES)
    xtail_shape = (X_SLABS, CONV_TAIL, LANES)
    utail_shape = (U_SLABS, MAX_WINDOW, LANES)
    s0, xtail0, utail0 = pl.pallas_call(
        _meta_kernel,
        out_shape=(jax.ShapeDtypeStruct(state_shape, F32),
                   jax.ShapeDtypeStruct(xtail_shape, F32),
                   jax.ShapeDtypeStruct(utail_shape, F32)),
        scratch_shapes=[
            pltpu.VMEM((X_SLABS, CHUNK + CONV_TAIL, LANES), F32),
            pltpu.VMEM((CHUNK, D_SSM), F32),
            pltpu.VMEM((CHUNK, D_SSM), BF16),
            pltpu.VMEM((CHUNK, N_GROUPS * D_STATE), F32),
            pltpu.VMEM((CHUNK, N_GROUPS * D_STATE), F32),
        ],
        compiler_params=pltpu.CompilerParams(vmem_limit_bytes=VMEM_LIMIT),
        name="meta_prologue",
    )(meta_pad, nw, win, wdt, convw, convb, dtb, alog)

    tm = TM_MIX
    tiles_per_seq = seq // tm
    n_tiles = bsz * tiles_per_seq

    def tile_index(m):
        m = jnp.clip(m, 0, n_tiles - 1)
        return (m // tiles_per_seq, m % tiles_per_seq, 0)

    return pl.pallas_call(
        functools.partial(_layer_kernel, tiles_per_seq=tiles_per_seq),
        out_shape=jax.ShapeDtypeStruct((bsz, seq, D_MODEL), F32),
        grid=(n_tiles + 1,),
        in_specs=[
            pl.BlockSpec((1, tm, D_MODEL), lambda n: tile_index(n + 1)),
            pl.BlockSpec((1, tm, D_MODEL), lambda n: tile_index(n)),
            _const_spec((1, D_MODEL)),
            _const_spec((D_MODEL, OFF_DT)),
            _const_spec((D_MODEL, DT_COLS)),
            _const_spec((N_GROUPS, POOL_GROUP, POOL_GROUP)),
            _const_spec((1, D_POOL)),
            _const_spec((CONV_WIDTH, D_XBC)),
            _const_spec((1, D_XBC)),
            _const_spec((1, DT_COLS)),
            _const_spec((1, DT_COLS)),
            _const_spec((1, D_SSM)),
            _const_spec((1, D_SSM)),
            *col_block_specs(D_MIX, D_MODEL),
            _const_spec(state_shape),
            _const_spec(xtail_shape),
            _const_spec(utail_shape),
            _const_spec((1, D_MODEL)),
            *col_block_specs(D_MODEL, D_FF),
            *col_block_specs(D_FF, D_MODEL),
            _const_spec((1, D_MODEL)),
        ],
        out_specs=pl.BlockSpec((1, tm, D_MODEL), lambda n: tile_index(n - 1)),
        scratch_shapes=[pltpu.VMEM(state_shape, F32)] + 2 * [
            pltpu.VMEM((U_SLABS, tm + MAX_WINDOW, LANES), F32),
            pltpu.VMEM((X_SLABS, tm + CONV_TAIL, LANES), F32),
            pltpu.VMEM((tm, D_SSM), F32),
            pltpu.VMEM((tm, DT_COLS), F32),
        ] + [
            pltpu.VMEM((tm, D_SSM), F32),
            pltpu.VMEM((tm, D_SSM), BF16),
            pltpu.VMEM((tm, N_GROUPS * D_STATE), F32),
            pltpu.VMEM((tm, N_GROUPS * D_STATE), F32),
            pltpu.VMEM((tm, D_MIX), BF16),
            pltpu.VMEM((tm, D_MODEL), F32),
            pltpu.VMEM((tm, D_MODEL), F32),
        ],
        compiler_params=pltpu.CompilerParams(
            dimension_semantics=("arbitrary",), vmem_limit_bytes=VMEM_LIMIT),
        name="layer",
    )(x, x, nw, win, wdt, pool_w[0].astype(BF16), row(pool_scale[0]), convw, convb, dtb, alog, dskip,
      row(ssm_norm_w[0]), *[wout_b] * (D_MODEL // W_COL_BLOCK), s0, xtail0, utail0,
      row(norm_ffn_w[0]), *[w1_b] * (D_FF // W_COL_BLOCK), *[w2_b] * (D_MODEL // W_COL_BLOCK),
      row(norm_f_w))
```

```python
import functools

import jax
import jax.numpy as jnp
from jax import lax
from jax.experimental import pallas as pl
from jax.experimental.pallas import tpu as pltpu

D_MODEL = 1024
N_META = 16
POOL_WINDOWS = (2, 4, 8, 16)
D_POOL = 512
POOL_GROUP = 128
MAX_WINDOW = 16
D_SSM = 1536
HEAD_DIM = 64
N_HEADS = 24
N_GROUPS = 4
HEADS_PER_GROUP = 6
D_STATE = 128
CONV_WIDTH = 4
CHUNK = 128
D_XBC = 2560
D_MIX = 2048
D_FF = 4096
EPS = 1e-5

LANES = 128
SUBLANES = 8
DT_COLS = LANES
D_IN_PAD = D_POOL + D_SSM + D_XBC + DT_COLS
OFF_Z = D_POOL
OFF_XBC = D_POOL + D_SSM
OFF_DT = OFF_XBC + D_XBC
N_PAIRS = N_HEADS // 2
PAIRS_PER_GROUP = HEADS_PER_GROUP // 2
GROUP_WIDTH = HEADS_PER_GROUP * HEAD_DIM
CONV_TAIL = SUBLANES
CONV_COL_BLOCK = 512
GATE_COL_BLOCK = 512
X_SLABS = D_XBC // LANES
U_SLABS = D_POOL // LANES

TM_MIX = 256
W_COL_BLOCK = 512
FF_BLOCK = W_COL_BLOCK
HEAD_FILL = (2, 1) + (2, 1, 2, 1, 1)
assert len(HEAD_FILL) == 2 + D_XBC // CONV_COL_BLOCK
VMEM_LIMIT = 62 * 1024 * 1024

F32 = jnp.float32
BF16 = jnp.bfloat16
LOG2E = 1.4426950408889634


def _dot(a, b):
    return jnp.dot(a, b, preferred_element_type=F32)


def _dot_split_n(a, w_refs, rows):
    return jnp.concatenate([_dot(a, w_ref[rows, :]) for w_ref in w_refs], axis=1)


def _rms(x, w):
    return x * lax.rsqrt(jnp.mean(x * x, axis=-1, keepdims=True) + EPS) * w


def _silu(x):
    return x * (1.0 / (1.0 + jnp.exp(-x)))


def _softplus(x):
    return jnp.maximum(x, 0.0) + jnp.log(1.0 + jnp.exp(-jnp.abs(x)))


def _split3(x):
    hi = x.astype(BF16)
    r1 = x - hi.astype(F32)
    mid = r1.astype(BF16)
    lo = (r1 - mid.astype(F32)).astype(BF16)
    return hi, mid, lo


def _bcast_rows(row):
    return jnp.broadcast_to(row, (CHUNK, LANES))


def _causal_mask():
    row_id = lax.broadcasted_iota(jnp.int32, (CHUNK, CHUNK), 0)
    col_id = lax.broadcasted_iota(jnp.int32, (CHUNK, CHUNK), 1)
    return row_id >= col_id


def _ssd_decays(dt_chunks, a_row):
    ltri = jnp.where(_causal_mask(), 1.0, 0.0).astype(BF16)
    terms = [t for dt_c in dt_chunks for t in _split3(dt_c * (a_row * LOG2E))]
    sums = _dot(ltri, jnp.concatenate(terms, axis=1))
    out = []
    for c, dt_c in enumerate(dt_chunks):
        hi, mid, lo = (sums[:, (3 * c + k) * LANES:(3 * c + k + 1) * LANES] for k in range(3))
        a_cs = hi + mid + lo
        a_cs_t = a_cs.T
        src_t = a_cs_t - jnp.log2(dt_c.T)
        out.append((a_cs, src_t, jnp.exp2(a_cs_t[:, CHUNK - 1:CHUNK] - src_t)))
    return out


def _ssd_chunk(decays, xs_f32, xs_bf, b_c, c_c, z_c, dskip_ref, ssmw_ref, s_ref, y_out,
               between=lambda: None):
    a_cs, src_t, to_end_t = decays
    causal = _causal_mask()
    lower_half = lax.broadcasted_iota(jnp.int32, (CHUNK, LANES), 1) < HEAD_DIM

    for g in range(N_GROUPS):
        gs = slice(g * D_STATE, (g + 1) * D_STATE)
        b_g = b_c[:, gs]
        c_g = c_c[:, gs]
        b_t = b_g.T
        if y_out is not None:
            cb = lax.dot_general(c_g.astype(BF16), b_g.astype(BF16),
                                 (((1,), (1,)), ((), ())), preferred_element_type=F32)
        y_pairs = []
        for q in range(PAIRS_PER_GROUP):
            between()
            pair = g * PAIRS_PER_GROUP + q
            ps = slice(pair * LANES, (pair + 1) * LANES)
            x_pair = xs_bf[:, ps]
            s_old = s_ref[pair]
            rhs = jnp.concatenate([x_pair, s_old.astype(BF16)], axis=0)
            b_scaled, lhs, end_decay = [], [], []
            for half in range(2):
                h = 2 * pair + half
                col_b = jnp.broadcast_to(a_cs[:, h:h + 1], (CHUNK, LANES))
                end_decay.append(jnp.exp2(col_b[CHUNK - 1:CHUNK, :]))
                b_scaled.append((b_t * _bcast_rows(to_end_t[h:h + 1, :])).astype(BF16))
                if y_out is not None:
                    seg = jnp.where(causal, col_b - _bcast_rows(src_t[h:h + 1, :]), -jnp.inf)
                    m_h = cb * jnp.exp2(seg)
                    c_h = c_g * jnp.exp2(col_b)
                    lhs.append(jnp.concatenate([m_h.astype(BF16), c_h.astype(BF16)], axis=1))
            if y_out is not None:
                y_both = _dot(jnp.concatenate(lhs, axis=0), rhs)
                y_full = (y_both[0:CHUNK, :], y_both[CHUNK:2 * CHUNK, :])
            zeros = jnp.zeros_like(x_pair)
            x_split = jnp.concatenate([jnp.where(lower_half, x_pair, zeros),
                                       jnp.where(lower_half, zeros, x_pair)], axis=0)
            st = _dot(jnp.concatenate(b_scaled, axis=1), x_split)
            s_ref[pair] = s_old * jnp.where(lower_half[0:1, :], end_decay[0], end_decay[1]) + st
            if y_out is not None:
                y_pairs.append(jnp.where(lower_half, y_full[0], y_full[1])
                               + dskip_ref[:, ps] * xs_f32[:, ps])
        if y_out is not None:
            ws = slice(g * GROUP_WIDTH, (g + 1) * GROUP_WIDTH)
            y_g = jnp.concatenate(y_pairs, axis=1) * _silu(z_c[:, ws])
            y_g = y_g * lax.rsqrt(jnp.mean(y_g * y_g, axis=-1, keepdims=True) + EPS) * ssmw_ref[:, ws]
            y_out(g, y_g.astype(BF16))


def _project_xbc(hn, tm, win_ref, xbuf, base, cb):
    res = _dot(hn, win_ref[:, OFF_XBC + cb:OFF_XBC + cb + CONV_COL_BLOCK])
    for i in range(0, CONV_COL_BLOCK, LANES):
        xbuf[base + (cb + i) // LANES, CONV_TAIL:CONV_TAIL + tm, :] = res[:, i:i + LANES]


def _conv_block(tm, convw_ref, convb_ref, xbuf, base, cb, xs_f32, xs_bf, b_buf, c_buf):
    for c0 in range(cb, cb + CONV_COL_BLOCK, LANES):
        cols = slice(c0, c0 + LANES)
        slab = base + c0 // LANES
        acc = convb_ref[:, cols] + convw_ref[CONV_WIDTH - 1:CONV_WIDTH, cols] * xbuf[slab, CONV_TAIL:CONV_TAIL + tm, :]
        for k in range(CONV_WIDTH - 1):
            start = CONV_TAIL - (CONV_WIDTH - 1) + k
            acc = acc + convw_ref[k:k + 1, cols] * xbuf[slab, start:start + tm, :]
        v = _silu(acc)
        if c0 < D_SSM:
            xs_f32[:, cols] = v
            if xs_bf is not None:
                xs_bf[:, cols] = v.astype(BF16)
        elif c0 < D_SSM + N_GROUPS * D_STATE:
            b_buf[:, c0 - D_SSM:c0 - D_SSM + LANES] = v
        else:
            off = c0 - D_SSM - N_GROUPS * D_STATE
            c_buf[:, off:off + LANES] = v


def _dt_and_a(dt_raw, dtb_ref, alog_ref):
    return _softplus(dt_raw + dtb_ref[...]), -jnp.exp(alog_ref[...])


def _meta_kernel(x_ref, nw_ref, win_ref, convw_ref, convb_ref, dtb_ref, alog_ref,
                 s_out, xtail_out, utail_out,
                 xbuf, xs_f32, xs_bf, b_buf, c_buf):
    s_out[...] = jnp.zeros_like(s_out)
    xbuf[:, 0:CONV_TAIL, :] = jnp.zeros((X_SLABS, CONV_TAIL, LANES), F32)
    hn = _rms(x_ref[...], nw_ref[...]).astype(BF16)
    u = _dot(hn, win_ref[:, 0:D_POOL])
    for gi in range(U_SLABS):
        utail_out[gi] = u[CHUNK - MAX_WINDOW:CHUNK, gi * POOL_GROUP:(gi + 1) * POOL_GROUP]
    for cb in range(0, D_XBC, CONV_COL_BLOCK):
        _project_xbc(hn, CHUNK, win_ref, xbuf, 0, cb)
        _conv_block(CHUNK, convw_ref, convb_ref, xbuf, 0, cb, xs_f32, xs_bf, b_buf, c_buf)
    xtail_out[...] = xbuf[:, CHUNK:CHUNK + CONV_TAIL, :]
    dt, a_row = _dt_and_a(_dot(hn, win_ref[:, OFF_DT:D_IN_PAD]), dtb_ref, alog_ref)
    valid = lax.broadcasted_iota(jnp.int32, (CHUNK, DT_COLS), 0) >= CHUNK - N_META
    dt = jnp.where(valid, dt, 0.0)
    _ssd_chunk(_ssd_decays([dt], a_row)[0], None, xs_bf[...], b_buf[...], c_buf[...], None, None, None, s_out, None)


def _spread(n_items, n_slots):
    return [-(-(i + 1) * n_items // n_slots) + (-i * n_items // n_slots) for i in range(n_slots)]


def _in_projection(hn, tm, win_ref, slot):
    ubuf, xbuf, zbuf, dtbuf = slot

    def dt_unit():
        dtbuf[...] = _dot(hn, win_ref[:, OFF_DT:D_IN_PAD])

    def pool_unit():
        u = _dot(hn, win_ref[:, 0:D_POOL])
        for gi in range(U_SLABS):
            ubuf[gi, MAX_WINDOW:MAX_WINDOW + tm, :] = u[:, gi * LANES:(gi + 1) * LANES]

    def gate_unit(j):
        zbuf[:, j:j + GATE_COL_BLOCK] = _dot(hn, win_ref[:, OFF_Z + j:OFF_Z + j + GATE_COL_BLOCK])

    units = [dt_unit, pool_unit]
    units += [functools.partial(_project_xbc, hn, tm, win_ref, xbuf, 0, cb)
              for cb in range(0, D_XBC, CONV_COL_BLOCK)]
    units += [functools.partial(gate_unit, j) for j in range(0, D_SSM, GATE_COL_BLOCK)]
    return units


def _layer_kernel(*refs, tiles_per_seq):
    refs = list(refs)
    take = lambda k: [refs.pop(0) for _ in range(k)]
    (xn_ref, xc_ref, nw_ref, win_ref, poolw_ref, pscale_ref, convw_ref, convb_ref, dtb_ref,
     alog_ref, dskip_ref, ssmw_ref) = take(12)
    wout_ref = take(D_MODEL // W_COL_BLOCK)
    s0_ref, xtail0_ref, utail0_ref, nw_ffn_ref = take(4)
    w1_ref = take(D_FF // W_COL_BLOCK)
    w2_ref = take(D_MODEL // W_COL_BLOCK)
    nfw_ref, o_ref = take(2)
    (s_ref, ubuf0, xbuf0, zbuf0, dtbuf0, ubuf1, xbuf1, zbuf1, dtbuf1,
     xs_f32, b_buf, c_buf, y_buf, h1_buf0, h1_buf1) = refs
    xs_bf = None
    n = pl.program_id(0)
    tm = TM_MIX
    slots = ((ubuf0, xbuf0, zbuf0, dtbuf0), (ubuf1, xbuf1, zbuf1, dtbuf1))
    x_even = xc_ref.at[:, pl.ds(0, tm), :]
    x_odd = xc_ref.at[:, pl.ds(tm, tm), :]

    @pl.when(n == 0)
    def _():
        h1_buf0[...] = jnp.zeros_like(h1_buf0)
        h1_buf1[...] = jnp.zeros_like(h1_buf1)
        ubuf1[...] = jnp.zeros_like(ubuf1)
        xbuf1[...] = jnp.zeros_like(xbuf1)
        hn0 = _rms(x_even[0], nw_ref[...]).astype(BF16)
        for unit in _in_projection(hn0, TM_MIX, win_ref, slots[0]):
            unit()

    first = n % (tiles_per_seq // 2) == 0

    @pl.when(first)
    def _():
        s_ref[...] = s0_ref[...]

    def step(first, cur, nxt, xn, xc, o, h1_buf):
        _layer_step(first, cur, nxt, xn, xc, nw_ref, win_ref, poolw_ref, pscale_ref, convw_ref, convb_ref,
                    dtb_ref, alog_ref, dskip_ref, ssmw_ref, wout_ref, xtail0_ref, utail0_ref,
                    nw_ffn_ref, w1_ref, w2_ref, nfw_ref, o,
                    s_ref, xs_f32, xs_bf, b_buf, c_buf, y_buf, h1_buf, o.at[0])

    step(first, slots[0], slots[1], x_odd, x_even, o_ref.at[:, pl.ds(0, tm), :], h1_buf0)
    step(False, slots[1], slots[0], xn_ref, x_odd, o_ref.at[:, pl.ds(tm, tm), :], h1_buf1)


def _layer_step(first, cur, nxt, xn_ref, xc_ref, nw_ref, win_ref, poolw_ref, pscale_ref, convw_ref,
                convb_ref, dtb_ref, alog_ref, dskip_ref, ssmw_ref, wout_ref, xtail0_ref, utail0_ref,
                nw_ffn_ref, w1_ref, w2_ref, nfw_ref, o_ref,
                s_ref, xs_f32, xs_bf, b_buf, c_buf, y_buf, h1_buf, h2_buf):
    tm = TM_MIX
    ubuf, xbuf, zbuf, dtbuf = cur

    xbuf[:, 0:CONV_TAIL, :] = jnp.where(first, xtail0_ref[...], nxt[1][:, tm:tm + CONV_TAIL, :])
    ubuf[:, 0:MAX_WINDOW, :] = jnp.where(first, utail0_ref[...], nxt[0][:, tm:tm + MAX_WINDOW, :])

    hn_next = _rms(xn_ref[0], nw_ref[...]).astype(BF16)
    h_prev = h1_buf[...]
    hn_prev = _rms(h_prev, nw_ffn_ref[...]).astype(BF16)
    h2_buf[...] = h_prev
    x = xc_ref[0]
    acts = {}

    def mlp_up(j):
        acts[j] = jnp.square(jnp.maximum(_dot(hn_prev, w1_ref[j][...]), 0.0)).astype(BF16)

    def mlp_down(j):
        h2_buf[...] += _dot_split_n(acts.pop(j), w2_ref, slice(j * FF_BLOCK, (j + 1) * FF_BLOCK))

    n_ff = D_FF // FF_BLOCK
    mlp_units = [functools.partial(mlp_up, 0)]
    for j in range(1, n_ff):
        mlp_units += [functools.partial(mlp_up, j), functools.partial(mlp_down, j - 1)]
    mlp_units.append(functools.partial(mlp_down, n_ff - 1))
    proj_units = _in_projection(hn_next, tm, win_ref, nxt)
    fillers = []
    for i, k in enumerate(_spread(len(proj_units), len(mlp_units))):
        fillers.append(mlp_units[i])
        fillers += [proj_units.pop(0) for _ in range(k)]
    n_pairs = (tm // CHUNK) * N_PAIRS
    quota = iter(HEAD_FILL + tuple(_spread(len(fillers) - sum(HEAD_FILL), n_pairs)))

    def fill():
        for _ in range(next(quota)):
            fillers.pop(0)()

    fill()
    dt, a_row = _dt_and_a(dtbuf[...], dtb_ref, alog_ref)
    decays = _ssd_decays([dt[c * CHUNK:(c + 1) * CHUNK, :] for c in range(tm // CHUNK)], a_row)

    fill()
    for gi, w in enumerate(POOL_WINDOWS):
        cols = slice(gi * POOL_GROUP, (gi + 1) * POOL_GROUP)
        tok = ubuf[gi, MAX_WINDOW:MAX_WINDOW + tm, :]
        acc = tok
        for j in range(1, w):
            acc = acc + ubuf[gi, MAX_WINDOW - j:MAX_WINDOW - j + tm, :]
        pooled = acc * (1.0 / w) - tok
        mixed = _dot(pooled.astype(BF16), poolw_ref[gi])
        y_buf[:, cols] = (mixed * pscale_ref[:, cols]).astype(BF16)

    for cb in range(0, D_XBC, CONV_COL_BLOCK):
        fill()
        _conv_block(tm, convw_ref, convb_ref, xbuf, 0, cb, xs_f32, xs_bf, b_buf, c_buf)

    for c in range(tm // CHUNK):
        rows = slice(c * CHUNK, (c + 1) * CHUNK)

        def store_y(g, val, rows=rows):
            y_buf[rows, D_POOL + g * GROUP_WIDTH:D_POOL + (g + 1) * GROUP_WIDTH] = val

        _ssd_chunk(decays[c], xs_f32[rows, :], xs_f32[rows, :].astype(BF16), b_buf[rows, :], c_buf[rows, :],
                   zbuf[rows, :], dskip_ref, ssmw_ref, s_ref, store_y, between=fill)
        h1_buf[rows, :] = x[rows, :] + _dot_split_n(y_buf[rows, :], wout_ref, slice(None))

    assert not fillers and not acts and next(quota, None) is None
    o_ref[0] = _rms(h2_buf[...], nfw_ref[...])


def _const_spec(shape):
    nd = len(shape)
    return pl.BlockSpec(shape, lambda *_: (0,) * nd, pipeline_mode=pl.Buffered(1))


def kernel(x, meta, norm_mix_w, w_in, pool_w, pool_scale, conv_w, conv_b, dt_bias, a_log, d_skip,
           ssm_norm_w, w_out, norm_ffn_w, w_ff1, w_ff2, norm_f_w):
    bsz, seq, d_model = x.shape
    assert d_model == D_MODEL and seq % TM_MIX == 0
    assert w_in.shape == (1, D_MODEL, OFF_DT + N_HEADS) and meta.shape == (N_META, D_MODEL)

    win = jnp.concatenate([w_in[0], jnp.zeros((D_MODEL, DT_COLS - N_HEADS), F32)], axis=1).astype(BF16)
    pad_heads = lambda v: jnp.concatenate([v, jnp.zeros((DT_COLS - N_HEADS,), F32)])[None, :]
    dtb = pad_heads(dt_bias[0])
    alog = pad_heads(a_log[0])
    dskip = jnp.repeat(d_skip[0], HEAD_DIM)[None, :]
    row = lambda v: v[None, :]
    nw = row(norm_mix_w[0])
    convw = conv_w[0]
    convb = row(conv_b[0])
    meta_pad = jnp.concatenate([jnp.zeros((CHUNK - N_META, D_MODEL), F32), meta.astype(F32)], axis=0)

    def col_block_specs(k, n):
        return [pl.BlockSpec((k, W_COL_BLOCK), lambda _, j=j: (0, j), pipeline_mode=pl.Buffered(1))
                for j in range(n // W_COL_BLOCK)]

    wout_b, w1_b, w2_b = w_out[0].astype(BF16), w_ff1[0].astype(BF16), w_ff2[0].astype(BF16)

    state_shape = (N_PAIRS, D_STATE, LANES)
    xtail_shape = (X_SLABS, CONV_TAIL, LANES)
    utail_shape = (U_SLABS, MAX_WINDOW, LANES)
    s0, xtail0, utail0 = pl.pallas_call(
        _meta_kernel,
        out_shape=(jax.ShapeDtypeStruct(state_shape, F32),
                   jax.ShapeDtypeStruct(xtail_shape, F32),
                   jax.ShapeDtypeStruct(utail_shape, F32)),
        scratch_shapes=[
            pltpu.VMEM((X_SLABS, CHUNK + CONV_TAIL, LANES), F32),
            pltpu.VMEM((CHUNK, D_SSM), F32),
            pltpu.VMEM((CHUNK, D_SSM), BF16),
            pltpu.VMEM((CHUNK, N_GROUPS * D_STATE), F32),
            pltpu.VMEM((CHUNK, N_GROUPS * D_STATE), F32),
        ],
        compiler_params=pltpu.CompilerParams(vmem_limit_bytes=VMEM_LIMIT),
        name="meta_prologue",
    )(meta_pad, nw, win, convw, convb, dtb, alog)

    tm = TM_MIX
    tiles_per_seq = seq // tm
    n_tiles = bsz * tiles_per_seq

    assert tiles_per_seq % 2 == 0
    pairs_per_seq = tiles_per_seq // 2
    n_pairs = n_tiles // 2

    def tile_index(m):
        m = jnp.clip(m, 0, n_tiles - 1)
        return (m // tiles_per_seq, m % tiles_per_seq, 0)

    def pair_index(m):
        m = jnp.clip(m, 0, n_pairs - 1)
        return (m // pairs_per_seq, m % pairs_per_seq, 0)

    return pl.pallas_call(
        functools.partial(_layer_kernel, tiles_per_seq=tiles_per_seq),
        out_shape=jax.ShapeDtypeStruct((bsz, seq, D_MODEL), F32),
        grid=(n_pairs + 1,),
        in_specs=[
            pl.BlockSpec((1, tm, D_MODEL), lambda n: tile_index(2 * n + 2)),
            pl.BlockSpec((1, 2 * tm, D_MODEL), lambda n: pair_index(n)),
            _const_spec((1, D_MODEL)),
            _const_spec((D_MODEL, D_IN_PAD)),
            _const_spec((N_GROUPS, POOL_GROUP, POOL_GROUP)),
            _const_spec((1, D_POOL)),
            _const_spec((CONV_WIDTH, D_XBC)),
            _const_spec((1, D_XBC)),
            _const_spec((1, DT_COLS)),
            _const_spec((1, DT_COLS)),
            _const_spec((1, D_SSM)),
            _const_spec((1, D_SSM)),
            *col_block_specs(D_MIX, D_MODEL),
            _const_spec(state_shape),
            _const_spec(xtail_shape),
            _const_spec(utail_shape),
            _const_spec((1, D_MODEL)),
            *col_block_specs(D_MODEL, D_FF),
            *col_block_specs(D_FF, D_MODEL),
            _const_spec((1, D_MODEL)),
        ],
        out_specs=pl.BlockSpec((1, 2 * tm, D_MODEL), lambda n: pair_index(n - 1)),
        scratch_shapes=[pltpu.VMEM(state_shape, F32)] + 2 * [
            pltpu.VMEM((U_SLABS, tm + MAX_WINDOW, LANES), F32),
            pltpu.VMEM((X_SLABS, tm + CONV_TAIL, LANES), F32),
            pltpu.VMEM((tm, D_SSM), F32),
            pltpu.VMEM((tm, DT_COLS), F32),
        ] + [
            pltpu.VMEM((tm, D_SSM), F32),
            pltpu.VMEM((tm, N_GROUPS * D_STATE), F32),
            pltpu.VMEM((tm, N_GROUPS * D_STATE), F32),
            pltpu.VMEM((tm, D_MIX), BF16),
            pltpu.VMEM((tm, D_MODEL), F32),
            pltpu.VMEM((tm, D_MODEL), F32),
        ],
        compiler_params=pltpu.CompilerParams(
            dimension_semantics=("arbitrary",), vmem_limit_bytes=VMEM_LIMIT),
        name="layer",
    )(x, x, nw, win, pool_w[0].astype(BF16), row(pool_scale[0]), convw, convb, dtb, alog, dskip,
      row(ssm_norm_w[0]), *[wout_b] * (D_MODEL // W_COL_BLOCK), s0, xtail0, utail0,
      row(norm_ffn_w[0]), *[w1_b] * (D_FF // W_COL_BLOCK), *[w2_b] * (D_MODEL // W_COL_BLOCK),
      row(norm_f_w))
```
